```python
import math
import jax, jax.numpy as jnp
from jax import lax
import numpy as np

D_MODEL = 2048
BATCH = 4
SEQ = 2048
DEPTH = 2

D_MIX = D_MODEL
D_LRU = D_MIX // 2
LRU_HEADS = 4
LRU_HEAD_DIM = D_LRU // LRU_HEADS
CONV_WIDTH = 4
LRU_C = 8.0
D_POOL = D_MIX - D_LRU
POOL_WINDOWS = (2, 4, 8, 16)
N_POOL_GROUPS = len(POOL_WINDOWS)
POOL_GROUP_DIM = D_POOL // N_POOL_GROUPS
D_IN = 2 * D_LRU + D_POOL
D_FF = ((8 * D_MODEL // 3 + 255) // 256) * 256
N_EXPERTS = 8
TOP_K = 2
D_FF_EXPERT = 7 * D_MODEL // 2
N_DENSE = (DEPTH + 1) // 2
N_MOE = DEPTH // 2
DEEPNORM_ALPHA = (2 * DEPTH) ** 0.25
DEEPNORM_BETA = (8 * DEPTH) ** -0.25
LN_EPS = 1e-5

kernel_name = "hybrid_lru_pool_moe_deepnorm_adaln"


def layer_norm(x, g, b):
    xf = x.astype(jnp.float32)
    mu = jnp.mean(xf, axis=-1, keepdims=True)
    var = jnp.mean(jnp.square(xf - mu), axis=-1, keepdims=True)
    return ((xf - mu) * lax.rsqrt(var + LN_EPS) * g.astype(jnp.float32) + b.astype(jnp.float32)).astype(x.dtype)


def adaln_params(c_act, w, b):
    m = c_act @ w + b
    shift, scale, gate = jnp.split(m, 3, axis=-1)
    return shift[:, None, :], scale[:, None, :], gate[:, None, :]


def causal_depthwise_conv(x, w, b):
    S = x.shape[1]
    xp = jnp.pad(x, ((0, 0), (CONV_WIDTH - 1, 0), (0, 0)))
    y = b
    for k in range(CONV_WIDTH):
        y = y + xp[:, k:k + S] * w[k]
    return y


def rg_lru(x, w_a, b_a, w_x, b_x, lam):
    B, S, _ = x.shape
    xh = x.reshape(B, S, LRU_HEADS, LRU_HEAD_DIM)
    gate_a = jax.nn.sigmoid(jnp.einsum('bshi,hij->bshj', xh, w_a).reshape(B, S, D_LRU) + b_a)
    gate_x = jax.nn.sigmoid(jnp.einsum('bshi,hij->bshj', xh, w_x).reshape(B, S, D_LRU) + b_x)
    log_a = -LRU_C * gate_a.astype(jnp.float32) * jax.nn.softplus(-lam.astype(jnp.float32))
    a = jnp.exp(log_a)
    mult = jnp.sqrt(-jnp.expm1(2.0 * log_a))
    u = (x * gate_x).astype(jnp.float32) * mult

    def step(h, au):
        a_t, u_t = au
        h = a_t * h + u_t
        return h, h

    h0 = jnp.zeros((B, D_LRU), jnp.float32)
    _, hs = lax.scan(step, h0, (jnp.swapaxes(a, 0, 1), jnp.swapaxes(u, 0, 1)))
    return jnp.swapaxes(hs, 0, 1).astype(x.dtype)


def multiscale_pool(x, w_pool, b_pool, scale):
    B, S, _ = x.shape
    xg = x.astype(jnp.float32).reshape(B, S, N_POOL_GROUPS, POOL_GROUP_DIM)
    cs = jnp.cumsum(xg, axis=1)
    outs = []
    for gi, w in enumerate(POOL_WINDOWS):
        c_g = cs[:, :, gi]
        lag = jnp.pad(c_g, ((0, 0), (w, 0), (0, 0)))[:, :S]
        cnt = jnp.minimum(jnp.arange(1, S + 1), w).astype(jnp.float32)[None, :, None]
        outs.append((c_g - lag) / cnt - xg[:, :, gi])
    p = jnp.stack(outs, axis=2).astype(x.dtype)
    y = jnp.einsum('bsgi,gij->bsgj', p, w_pool) + b_pool
    return y.reshape(B, S, D_POOL) * scale


def hybrid_mixer(h, w_in, conv_w, conv_b, lru_wa, lru_ba, lru_wx, lru_bx, lru_lam,
                 pool_w, pool_b, pool_scale, w_out):
    z = h @ w_in
    x_lru, g_lru, x_pool = jnp.split(z, [D_LRU, 2 * D_LRU], axis=-1)
    x_lru = causal_depthwise_conv(x_lru, conv_w, conv_b)
    y_lru = rg_lru(x_lru, lru_wa, lru_ba, lru_wx, lru_bx, lru_lam) * jax.nn.gelu(g_lru)
    y_pool = multiscale_pool(x_pool, pool_w, pool_b, pool_scale)
    return jnp.concatenate([y_lru, y_pool], axis=-1) @ w_out


def swiglu(h, w_gate, w_up, w_down):
    return (jax.nn.silu(h @ w_gate) * (h @ w_up)) @ w_down


def moe_swiglu(h, w_router, w_gate, w_up, w_down):
    B, S, D = h.shape
    t = h.reshape(B * S, D)
    logits = (t @ w_router).astype(jnp.float32)
    top_v, top_i = lax.top_k(logits, TOP_K)
    probs = jax.nn.softmax(top_v, axis=-1)
    combine = jnp.einsum('nk,nke->ne', probs, jax.nn.one_hot(top_i, N_EXPERTS, dtype=jnp.float32))
    out = jnp.zeros_like(t)
    for e in range(N_EXPERTS):
        y = swiglu(t, w_gate[e], w_up[e], w_down[e])
        out = out + combine[:, e:e + 1].astype(t.dtype) * y
    return out.reshape(B, S, D)


def setup_inputs(seed: int = 0) -> dict:
    key = jax.random.key(seed)
    ks = jax.random.split(key, 26)
    n = lambda k, shape, s: jax.random.normal(k, shape, jnp.float32) * s
    u = jax.random.uniform(ks[13], (DEPTH, D_LRU), jnp.float32, 0.81, 0.998)
    p = u ** (1.0 / LRU_C)
    lru_lam = jnp.log(p) - jnp.log1p(-p)
    return {
        "x": n(ks[0], (BATCH, SEQ, D_MODEL), 1.0),
        "c": n(ks[1], (BATCH, D_MODEL), 1.0),
        "ada_w": n(ks[2], (DEPTH, 2, D_MODEL, 3 * D_MODEL), 0.1 * D_MODEL ** -0.5),
        "ada_b": n(ks[3], (DEPTH, 2, 3 * D_MODEL), 0.01),
        "ln_g": 1.0 + n(ks[4], (DEPTH, 2, D_MODEL), 0.02),
        "ln_b": n(ks[5], (DEPTH, 2, D_MODEL), 0.02),
        "mix_w_in": n(ks[6], (DEPTH, D_MODEL, D_IN), D_MODEL ** -0.5),
        "conv_w": n(ks[7], (DEPTH, CONV_WIDTH, D_LRU), CONV_WIDTH ** -0.5),
        "conv_b": n(ks[8], (DEPTH, D_LRU), 0.01),
        "lru_wa": n(ks[9], (DEPTH, LRU_HEADS, LRU_HEAD_DIM, LRU_HEAD_DIM), LRU_HEAD_DIM ** -0.5),
        "lru_ba": n(ks[10], (DEPTH, D_LRU), 0.01),
        "lru_wx": n(ks[11], (DEPTH, LRU_HEADS, LRU_HEAD_DIM, LRU_HEAD_DIM), LRU_HEAD_DIM ** -0.5),
        "lru_bx": n(ks[12], (DEPTH, D_LRU), 0.01),
        "lru_lam": lru_lam,
        "pool_w": n(ks[14], (DEPTH, N_POOL_GROUPS, POOL_GROUP_DIM, POOL_GROUP_DIM), POOL_GROUP_DIM ** -0.5),
        "pool_b": n(ks[15], (DEPTH, N_POOL_GROUPS, POOL_GROUP_DIM), 0.01),
        "pool_scale": 1.0 + n(ks[16], (DEPTH, D_POOL), 0.02),
        "mix_w_out": n(ks[17], (DEPTH, D_MIX, D_MODEL), DEEPNORM_BETA * D_MIX ** -0.5),
        "ffn_w_gate": n(ks[18], (N_DENSE, D_MODEL, D_FF), D_MODEL ** -0.5),
        "ffn_w_up": n(ks[19], (N_DENSE, D_MODEL, D_FF), D_MODEL ** -0.5),
        "ffn_w_down": n(ks[20], (N_DENSE, D_FF, D_MODEL), DEEPNORM_BETA * D_FF ** -0.5),
        "router_w": n(ks[21], (N_MOE, D_MODEL, N_EXPERTS), D_MODEL ** -0.5),
        "exp_w_gate": n(ks[22], (N_MOE, N_EXPERTS, D_MODEL, D_FF_EXPERT), D_MODEL ** -0.5),
        "exp_w_up": n(ks[23], (N_MOE, N_EXPERTS, D_MODEL, D_FF_EXPERT), D_MODEL ** -0.5),
        "exp_w_down": n(ks[24], (N_MOE, N_EXPERTS, D_FF_EXPERT, D_MODEL), DEEPNORM_BETA * D_FF_EXPERT ** -0.5),
    }


def reference(x, c, ada_w, ada_b, ln_g, ln_b, mix_w_in, conv_w, conv_b, lru_wa, lru_ba,
              lru_wx, lru_bx, lru_lam, pool_w, pool_b, pool_scale, mix_w_out,
              ffn_w_gate, ffn_w_up, ffn_w_down, router_w, exp_w_gate, exp_w_up, exp_w_down):
    c_act = jax.nn.silu(c)
    for l in range(DEPTH):
        shift, scale, gate = adaln_params(c_act, ada_w[l, 0], ada_b[l, 0])
        h = x * (1.0 + scale) + shift
        o = hybrid_mixer(h, mix_w_in[l], conv_w[l], conv_b[l], lru_wa[l], lru_ba[l],
                         lru_wx[l], lru_bx[l], lru_lam[l], pool_w[l], pool_b[l],
                         pool_scale[l], mix_w_out[l])
        x = layer_norm(DEEPNORM_ALPHA * x + (1.0 + gate) * o, ln_g[l, 0], ln_b[l, 0])
        shift, scale, gate = adaln_params(c_act, ada_w[l, 1], ada_b[l, 1])
        h = x * (1.0 + scale) + shift
        if l % 2 == 0:
            i = l // 2
            o = swiglu(h, ffn_w_gate[i], ffn_w_up[i], ffn_w_down[i])
        else:
            i = l // 2
            o = moe_swiglu(h, router_w[i], exp_w_gate[i], exp_w_up[i], exp_w_down[i])
        x = layer_norm(DEEPNORM_ALPHA * x + (1.0 + gate) * o, ln_g[l, 1], ln_b[l, 1])
    return x
```

```python
import functools

import jax
import jax.numpy as jnp
from jax import lax
from jax.experimental import pallas as pl
from jax.experimental.pallas import tpu as pltpu

F32 = jnp.float32
BF16 = jnp.bfloat16

LN_EPS = 1e-5
LRU_C = 8.0
CONV_WIDTH = 4
LRU_HEADS = 4
POOL_WINDOWS = (2, 4, 8, 16)
N_EXPERTS = 8
TOP_K = 2

V7X_VMEM_LIMIT_BYTES = 58 * 1024 * 1024
LANES = 128
SUBLANES = 8
POOL_HIST = 16


def _params(sem, vmem=V7X_VMEM_LIMIT_BYTES):
    return pltpu.CompilerParams(dimension_semantics=sem, vmem_limit_bytes=vmem)


def _ada_kernel(c_ref, w_ref, b_ref, o_ref):
    c = c_ref[...]
    c_act = (c * jax.nn.sigmoid(c)).astype(BF16)
    o_ref[...] = jnp.dot(c_act, w_ref[...].astype(BF16),
                         preferred_element_type=F32) + b_ref[...]


def _ada(c, ada_w, ada_b):
    depth, _, d, d3 = ada_w.shape
    nb = c.shape[0]
    rows = -(-nb // SUBLANES) * SUBLANES
    c_pad = jnp.pad(c, ((0, rows - nb), (0, 0)))
    w = ada_w.reshape(depth * 2, d, d3)
    b = ada_b.reshape(depth * 2, 1, d3)
    tn = 1024
    out = pl.pallas_call(
        _ada_kernel,
        grid=(depth * 2, d3 // tn),
        in_specs=[
            pl.BlockSpec((rows, d), lambda l, j: (0, 0)),
            pl.BlockSpec((None, d, tn), lambda l, j: (l, 0, j)),
            pl.BlockSpec((None, 1, tn), lambda l, j: (l, 0, j)),
        ],
        out_specs=pl.BlockSpec((None, rows, tn), lambda l, j: (l, 0, j)),
        out_shape=jax.ShapeDtypeStruct((depth * 2, rows, d3), F32),
        compiler_params=_params(("arbitrary", "arbitrary")),
        name="ada",
    )(c_pad, w, b)
    m = out[:, :nb, :].reshape(depth, 2, nb, 3, 1, d)
    return m


def _mix_in_kernel(x_ref, shift_ref, scale_ref, w_ref, z_ref, h_scr):
    @pl.when(pl.program_id(1) == 0)
    def _():
        h_scr[...] = (x_ref[...] * (1.0 + scale_ref[...]) + shift_ref[...]).astype(BF16)

    z_ref[...] = jnp.dot(h_scr[...], w_ref[...].astype(BF16), preferred_element_type=F32)


def _mix_in(x2, shift, scale, w_in, seq):
    n, d = x2.shape
    d_in = w_in.shape[1]
    tm, tn = 1024, 1024
    per_b = seq // tm
    return pl.pallas_call(
        _mix_in_kernel,
        grid=(n // tm, d_in // tn),
        in_specs=[
            pl.BlockSpec((tm, d), lambda i, j: (i, 0)),
            pl.BlockSpec((None, 1, d), lambda i, j: (i // per_b, 0, 0)),
            pl.BlockSpec((None, 1, d), lambda i, j: (i // per_b, 0, 0)),
            pl.BlockSpec((d, tn), lambda i, j: (0, j)),
        ],
        out_specs=pl.BlockSpec((tm, tn), lambda i, j: (i, j)),
        out_shape=jax.ShapeDtypeStruct((n, d_in), F32),
        scratch_shapes=[pltpu.VMEM((tm, d), BF16)],
        compiler_params=_params(("arbitrary", "arbitrary")),
        name="mix_in",
    )(x2, shift, scale, w_in)


def _block_diag_dot(v_bf16, w_ref, n_blocks):
    bd = w_ref.shape[-1]
    outs = [jnp.dot(v_bf16[:, h * bd:(h + 1) * bd], w_ref[h], preferred_element_type=F32)
            for h in range(n_blocks)]
    return jnp.concatenate(outs, axis=1)


def _neg_expm1_nonpos(v, exp_v):
    poly = 1.0 + v * (1.0 / 8.0)
    for k in (7.0, 6.0, 5.0, 4.0, 3.0, 2.0):
        poly = 1.0 + (v * (1.0 / k)) * poly
    return jnp.where(v > -0.1, -v * poly, 1.0 - exp_v)


def _mix_seq_kernel(xl_ref, gl_ref, xp_ref, cw_ref, cb_ref, wa_ref, ba_ref, wx_ref, bx_ref,
                    lam_ref, pw_ref, pb_ref, ps_ref, y_ref, conv_hist, pool_hist, h_carry, *, ts):
    t = pl.program_id(1)

    @pl.when(t == 0)
    def _():
        conv_hist[...] = jnp.zeros_like(conv_hist)
        pool_hist[...] = jnp.zeros_like(pool_hist)
        h_carry[...] = jnp.zeros_like(h_carry)

    d_lru = xl_ref.shape[1]
    row = lax.broadcasted_iota(jnp.int32, (ts, 1), 0)

    xl = xl_ref[...]
    buf = jnp.concatenate([conv_hist[...], xl], axis=0)
    cw = cw_ref[...]
    xc = cb_ref[...] + cw[CONV_WIDTH - 1:CONV_WIDTH] * xl
    for k in range(CONV_WIDTH - 1):
        back = CONV_WIDTH - 1 - k
        xc = xc + cw[k:k + 1] * pltpu.roll(buf, back, 0)[SUBLANES:]
    conv_hist[...] = xl[ts - SUBLANES:]

    xcb = xc.astype(BF16)
    gate_a = jax.nn.sigmoid(_block_diag_dot(xcb, wa_ref, LRU_HEADS) + ba_ref[...])
    gate_x = jax.nn.sigmoid(_block_diag_dot(xcb, wx_ref, LRU_HEADS) + bx_ref[...])
    log_a = (-LRU_C) * gate_a * jax.nn.softplus(-lam_ref[...])
    a = jnp.exp(log_a)
    mult = jnp.sqrt(_neg_expm1_nonpos(2.0 * log_a, a * a))
    u = (xc * gate_x) * mult

    d = 1
    while d < ts:
        keep = row >= d
        a_prev = jnp.where(keep, pltpu.roll(a, d, 0), 1.0)
        u_prev = jnp.where(keep, pltpu.roll(u, d, 0), 0.0)
        u = u + a * u_prev
        a = a * a_prev
        d *= 2
    hs = a * h_carry[0:1, :] + u
    h_carry[...] = jnp.broadcast_to(hs[ts - 1:ts, :], h_carry.shape)
    y_ref[:, :d_lru] = (hs * jax.nn.gelu(gl_ref[...])).astype(y_ref.dtype)

    xp = xp_ref[...]
    bufp = jnp.concatenate([pool_hist[...], xp], axis=0)
    pool_hist[...] = xp[ts - POOL_HIST:]
    t_glob = t * ts + row + 1
    gd = pw_ref.shape[-1]
    ps = []
    for gi, w in enumerate(POOL_WINDOWS):
        s = bufp[:, gi * gd:(gi + 1) * gd]
        sh = 1
        while sh < w:
            s = s + pltpu.roll(s, sh, 0)
            sh *= 2
        cnt = jnp.minimum(t_glob, w).astype(F32)
        ps.append(s[POOL_HIST:] / cnt - xp[:, gi * gd:(gi + 1) * gd])
    p = jnp.concatenate(ps, axis=1).astype(BF16)
    yp = (_block_diag_dot(p, pw_ref, len(POOL_WINDOWS)) + pb_ref[...]) * ps_ref[...]
    y_ref[:, d_lru:] = yp.astype(y_ref.dtype)


def _mix_seq(z, conv_w, conv_b, wa, ba, wx, bx, lam, pool_w, pool_b, pool_scale, nb, seq):
    n = z.shape[0]
    d_lru = conv_w.shape[1]
    d_pool = pool_scale.shape[0]
    assert d_lru == d_pool
    ts = 256
    nt = seq // ts
    row_vec = lambda v: v.reshape(1, -1)
    col_spec = lambda c: pl.BlockSpec((ts, d_lru), lambda b, t: (b * nt + t, c))
    full = lambda a: pl.BlockSpec(a.shape, lambda b, t: (0,) * a.ndim)
    small = [conv_w, row_vec(conv_b), wa.astype(BF16), row_vec(ba), wx.astype(BF16), row_vec(bx),
             row_vec(lam), pool_w.astype(BF16), row_vec(pool_b.reshape(-1)), row_vec(pool_scale)]
    return pl.pallas_call(
        functools.partial(_mix_seq_kernel, ts=ts),
        grid=(nb, nt),
        in_specs=[col_spec(0), col_spec(1), col_spec(2)] + [full(a) for a in small],
        out_specs=pl.BlockSpec((ts, d_lru + d_pool), lambda b, t: (b * nt + t, 0)),
        out_shape=jax.ShapeDtypeStruct((n, d_lru + d_pool), BF16),
        scratch_shapes=[pltpu.VMEM((SUBLANES, d_lru), F32),
                        pltpu.VMEM((POOL_HIST, d_pool), F32),
                        pltpu.VMEM((SUBLANES, d_lru), F32)],
        compiler_params=_params(("arbitrary", "arbitrary")),
        name="mix_seq",
    )(z, z, z, *small)


def _resid_ln(x, o, gate, g, b, alpha):
    v = alpha * x + (1.0 + gate) * o
    mu = jnp.mean(v, axis=-1, keepdims=True)
    dv = v - mu
    var = jnp.mean(dv * dv, axis=-1, keepdims=True)
    return dv * lax.rsqrt(var + LN_EPS) * g + b


def _route(h, wr_ref):
    logits = jnp.dot(h, wr_ref[...], preferred_element_type=F32, precision=lax.Precision.HIGHEST)
    lane = lax.broadcasted_iota(jnp.int32, logits.shape, 1).astype(F32)
    neg = jnp.float32(-jnp.inf)
    lg = jnp.where(lane < N_EXPERTS, logits, neg)
    m1 = jnp.max(lg, axis=-1, keepdims=True)
    i1 = jnp.min(jnp.where(lg == m1, lane, float(LANES)), axis=-1, keepdims=True)
    lg2 = jnp.where(lane == i1, neg, lg)
    m2 = jnp.max(lg2, axis=-1, keepdims=True)
    i2 = jnp.min(jnp.where(lg2 == m2, lane, float(LANES)), axis=-1, keepdims=True)
    e = jnp.exp(m2 - m1)
    p1 = 1.0 / (1.0 + e)
    p2 = e / (1.0 + e)
    idx = jnp.where(lane == 0, i1, i2).astype(jnp.int32)
    return idx, jnp.where(lane == 0, p1, p2)


def _mix_out_kernel(y_ref, x_ref, gate_ref, g_ref, b_ref, shift_ref, scale_ref, w_ref, *rest,
                    alpha, route):
    if route:
        wr_ref, xo_ref, h_ref, idx_ref, p_ref, w_scr = rest
    else:
        xo_ref, h_ref, w_scr = rest

    @pl.when(pl.program_id(0) == 0)
    def _():
        w_scr[...] = w_ref[...].astype(BF16)

    o = jnp.dot(y_ref[...], w_scr[...], preferred_element_type=F32)
    xn = _resid_ln(x_ref[...], o, gate_ref[...], g_ref[...], b_ref[...], alpha)
    xo_ref[...] = xn
    h = xn * (1.0 + scale_ref[...]) + shift_ref[...]
    h_ref[...] = h.astype(h_ref.dtype)
    if route:
        idx, p = _route(h, wr_ref)
        idx_ref[...] = idx
        p_ref[...] = p


def _mix_out(y, x2, gate, ln_g, ln_b, shift2, scale2, w_out, router_w, seq, alpha):
    n, d = x2.shape
    route = router_w is not None
    tm = 256
    per_b = seq // tm
    tok = lambda: pl.BlockSpec((tm, d), lambda i: (i, 0))
    mod = lambda: pl.BlockSpec((None, 1, d), lambda i: (i // per_b, 0, 0))
    vec = lambda: pl.BlockSpec((1, d), lambda i: (0, 0))
    in_specs = [tok(), tok(), mod(), vec(), vec(), mod(), mod(),
                pl.BlockSpec((d, d), lambda i: (0, 0), pipeline_mode=pl.Buffered(1))]
    args = [y, x2, gate, ln_g.reshape(1, d), ln_b.reshape(1, d), shift2, scale2, w_out]
    out_specs = [tok(), tok()]
    out_shape = [jax.ShapeDtypeStruct((n, d), F32),
                 jax.ShapeDtypeStruct((n, d), F32 if route else BF16)]
    if route:
        wr = jnp.pad(router_w, ((0, 0), (0, LANES - router_w.shape[1])))
        in_specs.append(pl.BlockSpec((d, LANES), lambda i: (0, 0)))
        args.append(wr)
        out_specs += [pl.BlockSpec((tm, LANES), lambda i: (i, 0))] * 2
        out_shape += [jax.ShapeDtypeStruct((n, LANES), jnp.int32),
                      jax.ShapeDtypeStruct((n, LANES), F32)]
    return pl.pallas_call(
        functools.partial(_mix_out_kernel, alpha=alpha, route=route),
        grid=(n // tm,),
        in_specs=in_specs,
        out_specs=out_specs,
        out_shape=out_shape,
        scratch_shapes=[pltpu.VMEM((d, d), BF16)],
        compiler_params=_params(("arbitrary",)),
        name="mix_out",
    )(*args)


def _ffn_kernel(te_ref, tr_ref, tb_ref, x_ref, wg_ref, wu_ref, wd_ref, *rest, ch, alpha, ln):
    if ln:
        xres_ref, gate_ref, g_ref, b_ref, o_ref = rest
    else:
        (o_ref,) = rest
    i = pl.program_id(0)
    j = pl.program_id(1)
    nj = pl.num_programs(1)
    rows = tr_ref[i]
    ts = x_ref.shape[0]
    n_valid = lax.div(rows + (ch - 1), ch)

    wg = wg_ref[...].astype(BF16)
    wu = wu_ref[...].astype(BF16)
    wd = wd_ref[...].astype(BF16)

    def chunk(c, carry):
        r0 = pl.multiple_of(c * ch, ch)
        x = x_ref[pl.ds(r0, ch), :]
        g = jnp.dot(x, wg, preferred_element_type=F32)
        u = jnp.dot(x, wu, preferred_element_type=F32)
        hmid = (g * jax.nn.sigmoid(g) * u).astype(BF16)
        y = jnp.dot(hmid, wd, preferred_element_type=F32)

        @pl.when(j == 0)
        def _():
            o_ref[pl.ds(r0, ch), :] = y

        @pl.when(j > 0)
        def _():
            o_ref[pl.ds(r0, ch), :] += y

        return carry

    lax.fori_loop(0, n_valid, chunk, 0)

    @pl.when(j == 0)
    def _():
        def zero(c, carry):
            r0 = pl.multiple_of(c * ch, ch)
            o_ref[pl.ds(r0, ch), :] = jnp.zeros((ch, o_ref.shape[1]), o_ref.dtype)
            return carry
        lax.fori_loop(n_valid, ts // ch, zero, 0)

    if ln:
        @pl.when(j == nj - 1)
        def _():
            def fin(c, carry):
                r0 = pl.multiple_of(c * ch, ch)
                o_ref[pl.ds(r0, ch), :] = _resid_ln(
                    xres_ref[pl.ds(r0, ch), :], o_ref[pl.ds(r0, ch), :],
                    gate_ref[...], g_ref[...], b_ref[...], alpha)
                return carry
            lax.fori_loop(0, ts // ch, fin, 0)


def _ffn(x_rows, w_gate, w_up, w_down, tile_expert, tile_rows, tile_block, *, ts, tf, ch,
         ln_args=None, seq=None, alpha=None):
    p_rows, d = x_rows.shape
    f = w_gate.shape[-1]
    n_tiles = tile_expert.shape[0]
    nj = f // tf
    ln = ln_args is not None

    def jj(i, j, tr):
        return jnp.where(tr[i] > 0, j, nj - 1)

    in_specs = [
        pl.BlockSpec((ts, d), lambda i, j, te, tr, tb: (tb[i], 0)),
        pl.BlockSpec((None, d, tf), lambda i, j, te, tr, tb: (te[i], 0, jj(i, j, tr))),
        pl.BlockSpec((None, d, tf), lambda i, j, te, tr, tb: (te[i], 0, jj(i, j, tr))),
        pl.BlockSpec((None, tf, d), lambda i, j, te, tr, tb: (te[i], jj(i, j, tr), 0)),
    ]
    args = [x_rows, w_gate, w_up, w_down]
    if ln:
        xres, gate, ln_g, ln_b = ln_args
        per_b = seq // ts
        in_specs += [
            pl.BlockSpec((ts, d), lambda i, j, te, tr, tb: (tb[i], 0),
                         pipeline_mode=pl.Buffered(1)),
            pl.BlockSpec((None, 1, d), lambda i, j, te, tr, tb: (tb[i] // per_b, 0, 0)),
            pl.BlockSpec((1, d), lambda i, j, te, tr, tb: (0, 0)),
            pl.BlockSpec((1, d), lambda i, j, te, tr, tb: (0, 0)),
        ]
        args += [xres, gate, ln_g.reshape(1, d), ln_b.reshape(1, d)]
    grid_spec = pltpu.PrefetchScalarGridSpec(
        num_scalar_prefetch=3,
        grid=(n_tiles, nj),
        in_specs=in_specs,
        out_specs=pl.BlockSpec((ts, d), lambda i, j, te, tr, tb: (i, 0)),
    )
    return pl.pallas_call(
        functools.partial(_ffn_kernel, ch=ch, alpha=alpha, ln=ln),
        grid_spec=grid_spec,
        out_shape=jax.ShapeDtypeStruct((p_rows, d), F32),
        compiler_params=_params(("arbitrary", "arbitrary")),
        name="ffn_ln" if ln else "ffn_moe",
    )(tile_expert, tile_rows, tile_block, *args)


def _row_copy(src_hbm, src_row, dst_buf, dst_row, sem):
    return pltpu.make_async_copy(src_hbm.at[pl.ds(src_row, 1)], dst_buf.at[pl.ds(dst_row, 1)], sem)


def _gather_kernel(tok_ref, used_ref, h_hbm, o_ref, buf, sem, *, tg):
    i = pl.program_id(0)
    n = pl.num_programs(0)

    def issue(tile, slot):
        def body(r, carry):
            _row_copy(h_hbm, tok_ref[tile * tg + r], buf.at[slot], r, sem.at[slot]).start()
            return carry
        lax.fori_loop(0, tg, body, 0, unroll=8)

    def drain(slot):
        def body(r, carry):
            _row_copy(h_hbm, 0, buf.at[slot], r, sem.at[slot]).wait()
            return carry
        lax.fori_loop(0, tg, body, 0, unroll=8)

    def live(tile):
        return tile * tg < used_ref[0]

    @pl.when(jnp.logical_and(i == 0, live(0)))
    def _():
        issue(0, 0)

    @pl.when(jnp.logical_and(i + 1 < n, live(i + 1)))
    def _():
        issue(i + 1, (i + 1) % 2)

    @pl.when(live(i))
    def _():
        drain(i % 2)
        o_ref[...] = buf[i % 2].astype(o_ref.dtype)

    @pl.when(jnp.logical_not(live(i)))
    def _():
        o_ref[...] = jnp.zeros_like(o_ref)


def _gather(h, tok_of_row, used_rows, p_rows):
    n, d = h.shape
    tg = 256
    grid_spec = pltpu.PrefetchScalarGridSpec(
        num_scalar_prefetch=2,
        grid=(p_rows // tg,),
        in_specs=[pl.BlockSpec(memory_space=pl.ANY)],
        out_specs=pl.BlockSpec((tg, d), lambda i, tok, used: (i, 0)),
        scratch_shapes=[pltpu.VMEM((2, tg, d), F32), pltpu.SemaphoreType.DMA((2,))],
    )
    return pl.pallas_call(
        functools.partial(_gather_kernel, tg=tg),
        grid_spec=grid_spec,
        out_shape=jax.ShapeDtypeStruct((p_rows, d), BF16),
        compiler_params=_params(("arbitrary",)),
        name="gather",
    )(tok_of_row, used_rows, h)


def _combine_kernel(dest_ref, x_ref, p_ref, gate_ref, g_ref, b_ref, ys_hbm, o_ref, buf, sem,
                    *, tc, alpha):
    i = pl.program_id(0)
    n = pl.num_programs(0)

    def issue(tile, slot):
        def body(r, carry):
            for k in range(TOP_K):
                src = dest_ref[(tile * tc + r) * TOP_K + k]
                _row_copy(ys_hbm, src, buf.at[slot, k], r, sem.at[slot]).start()
            return carry
        lax.fori_loop(0, tc, body, 0, unroll=4)

    def drain(slot):
        def body(r, carry):
            for k in range(TOP_K):
                _row_copy(ys_hbm, 0, buf.at[slot, k], r, sem.at[slot]).wait()
            return carry
        lax.fori_loop(0, tc, body, 0, unroll=4)

    @pl.when(i == 0)
    def _():
        issue(0, 0)

    @pl.when(i + 1 < n)
    def _():
        issue(i + 1, (i + 1) % 2)

    slot = i % 2
    drain(slot)
    p = p_ref[...]
    o = p[:, 0:1] * buf[slot, 0] + p[:, 1:2] * buf[slot, 1]
    o_ref[...] = _resid_ln(x_ref[...], o, gate_ref[...], g_ref[...], b_ref[...], alpha)


def _combine(ys, dest, x2, probs, gate, ln_g, ln_b, seq, alpha):
    n, d = x2.shape
    tc = 256
    per_b = seq // tc
    grid_spec = pltpu.PrefetchScalarGridSpec(
        num_scalar_prefetch=1,
        grid=(n // tc,),
        in_specs=[
            pl.BlockSpec((tc, d), lambda i, dst: (i, 0)),
            pl.BlockSpec((tc, LANES), lambda i, dst: (i, 0)),
            pl.BlockSpec((None, 1, d), lambda i, dst: (i // per_b, 0, 0)),
            pl.BlockSpec((1, d), lambda i, dst: (0, 0)),
            pl.BlockSpec((1, d), lambda i, dst: (0, 0)),
            pl.BlockSpec(memory_space=pl.ANY),
        ],
        out_specs=pl.BlockSpec((tc, d), lambda i, dst: (i, 0)),
        scratch_shapes=[pltpu.VMEM((2, TOP_K, tc, d), F32), pltpu.SemaphoreType.DMA((2,))],
    )
    return pl.pallas_call(
        functools.partial(_combine_kernel, tc=tc, alpha=alpha),
        grid_spec=grid_spec,
        out_shape=jax.ShapeDtypeStruct((n, d), F32),
        compiler_params=_params(("arbitrary",)),
        name="combine",
    )(dest, x2, probs, gate, ln_g.reshape(1, d), ln_b.reshape(1, d), ys)


def _routing_tables(top_i, ts, max_tiles):
    n = top_i.shape[0]
    e_flat = top_i.reshape(-1)
    onehot = (e_flat[:, None] == jnp.arange(N_EXPERTS, dtype=jnp.int32)[None, :]).astype(jnp.int32)
    csum = jnp.cumsum(onehot, axis=0)
    counts = csum[-1]
    rank = jnp.take_along_axis(csum, e_flat[:, None], axis=1)[:, 0] - 1
    tiles_per = (counts + ts - 1) // ts
    tile_end = jnp.cumsum(tiles_per)
    tile_start = tile_end - tiles_per
    dest = (tile_start * ts)[e_flat] + rank
    n_tiles = tile_end[-1]
    tidx = jnp.arange(max_tiles, dtype=jnp.int32)
    last = n_tiles - 1
    tclamp = jnp.minimum(tidx, last)
    tile_expert = jnp.sum((tile_end[None, :] <= tclamp[:, None]).astype(jnp.int32), axis=1)
    tile_rows = jnp.clip(counts[tile_expert] - (tclamp - tile_start[tile_expert]) * ts, 0, ts)
    tile_rows = jnp.where(tidx < n_tiles, tile_rows, 0)
    tok_of_row = jnp.zeros((max_tiles * ts,), jnp.int32).at[dest].set(
        jnp.arange(n * TOP_K, dtype=jnp.int32) // TOP_K)
    used_rows = (n_tiles * ts).reshape(1).astype(jnp.int32)
    return (dest.astype(jnp.int32), tok_of_row, used_rows,
            tile_expert.astype(jnp.int32), tile_rows.astype(jnp.int32), tclamp.astype(jnp.int32))


def kernel(x, c, ada_w, ada_b, ln_g, ln_b, mix_w_in, conv_w, conv_b, lru_wa, lru_ba, lru_wx, lru_bx,
           lru_lam, pool_w, pool_b, pool_scale, mix_w_out, ffn_w_gate, ffn_w_up, ffn_w_down,
           router_w, exp_w_gate, exp_w_up, exp_w_down):
    nb, seq, d = x.shape
    depth = ada_w.shape[0]
    n = nb * seq
    alpha = float((2 * depth) ** 0.25)
    mods = _ada(c, ada_w, ada_b)
    x2 = x.reshape(n, d)

    ffn_ts, ffn_tf, ffn_ch = 1024, 256, 256
    for l in range(depth):
        shift1, scale1, gate1 = mods[l, 0, :, 0], mods[l, 0, :, 1], mods[l, 0, :, 2]
        shift2, scale2, gate2 = mods[l, 1, :, 0], mods[l, 1, :, 1], mods[l, 1, :, 2]
        moe = (l % 2 == 1)
        i = l // 2

        z = _mix_in(x2, shift1, scale1, mix_w_in[l], seq)
        y = _mix_seq(z, conv_w[l], conv_b[l], lru_wa[l], lru_ba[l], lru_wx[l], lru_bx[l],
                     lru_lam[l], pool_w[l], pool_b[l], pool_scale[l], nb, seq)
        outs = _mix_out(y, x2, gate1, ln_g[l, 0], ln_b[l, 0], shift2, scale2, mix_w_out[l],
                        router_w[i] if moe else None, seq, alpha)
        if not moe:
            x2, h = outs
            n_tiles = n // ffn_ts
            te = jnp.full((n_tiles,), i, jnp.int32)
            tr = jnp.full((n_tiles,), ffn_ts, jnp.int32)
            tb = jnp.arange(n_tiles, dtype=jnp.int32)
            x2 = _ffn(h, ffn_w_gate, ffn_w_up, ffn_w_down, te, tr, tb,
                      ts=ffn_ts, tf=ffn_tf, ch=ffn_ch,
                      ln_args=(x2, gate2, ln_g[l, 1], ln_b[l, 1]), seq=seq, alpha=alpha)
        else:
            x2, h, idx, probs = outs
            max_tiles = (n * TOP_K) // ffn_ts + N_EXPERTS
            dest, tok_of_row, used_rows, te, tr, tb = _routing_tables(
                idx[:, :TOP_K], ffn_ts, max_tiles)
            xs = _gather(h, tok_of_row, used_rows, max_tiles * ffn_ts)
            n_exp = exp_w_gate.shape[1]
            f_exp = exp_w_gate.shape[-1]
            ys = _ffn(xs,
                      exp_w_gate.reshape(-1, d, f_exp), exp_w_up.reshape(-1, d, f_exp),
                      exp_w_down.reshape(-1, f_exp, d), te + i * n_exp, tr, tb,
                      ts=ffn_ts, tf=ffn_tf, ch=ffn_ch)
            x2 = _combine(ys, dest, x2, probs, gate2, ln_g[l, 1], ln_b[l, 1], seq, alpha)
    return x2.reshape(nb, seq, d)
```

```python
import functools

import jax
import jax.numpy as jnp
from jax import lax
from jax.experimental import pallas as pl
from jax.experimental.pallas import tpu as pltpu

F32 = jnp.float32
BF16 = jnp.bfloat16

LN_EPS = 1e-5
LRU_C = 8.0
CONV_WIDTH = 4
LRU_HEADS = 4
POOL_WINDOWS = (2, 4, 8, 16)
N_EXPERTS = 8
TOP_K = 2

V7X_VMEM_LIMIT_BYTES = 58 * 1024 * 1024
LANES = 128
SUBLANES = 8
POOL_HIST = 16


def _params(sem, vmem=V7X_VMEM_LIMIT_BYTES):
    return pltpu.CompilerParams(dimension_semantics=sem, vmem_limit_bytes=vmem)


def _ada_kernel(c_ref, w_ref, b_ref, o_ref):
    c = c_ref[...]
    c_act = (c * jax.nn.sigmoid(c)).astype(BF16)
    o_ref[...] = jnp.dot(c_act, w_ref[...].astype(BF16),
                         preferred_element_type=F32) + b_ref[...]


def _ada(c, ada_w, ada_b):
    depth, _, d, d3 = ada_w.shape
    nb = c.shape[0]
    rows = -(-nb // SUBLANES) * SUBLANES
    c_pad = jnp.pad(c, ((0, rows - nb), (0, 0)))
    w = ada_w.reshape(depth * 2, d, d3)
    b = ada_b.reshape(depth * 2, 1, d3)
    tn = 1024
    out = pl.pallas_call(
        _ada_kernel,
        grid=(depth * 2, d3 // tn),
        in_specs=[
            pl.BlockSpec((rows, d), lambda l, j: (0, 0)),
            pl.BlockSpec((None, d, tn), lambda l, j: (l, 0, j)),
            pl.BlockSpec((None, 1, tn), lambda l, j: (l, 0, j)),
        ],
        out_specs=pl.BlockSpec((None, rows, tn), lambda l, j: (l, 0, j)),
        out_shape=jax.ShapeDtypeStruct((depth * 2, rows, d3), F32),
        compiler_params=_params(("arbitrary", "arbitrary")),
        name="ada",
    )(c_pad, w, b)
    m = out[:, :nb, :].reshape(depth, 2, nb, 3, 1, d)
    return m


def _mix_in_kernel(x_ref, shift_ref, scale_ref, w_ref, z_ref, h_scr):
    @pl.when(pl.program_id(1) == 0)
    def _():
        h_scr[...] = (x_ref[...] * (1.0 + scale_ref[...]) + shift_ref[...]).astype(BF16)

    z_ref[...] = jnp.dot(h_scr[...], w_ref[...].astype(BF16), preferred_element_type=F32)


def _mix_in(x2, shift, scale, w_in, seq):
    n, d = x2.shape
    d_in = w_in.shape[1]
    tm, tn = 1024, 1024
    per_b = seq // tm
    return pl.pallas_call(
        _mix_in_kernel,
        grid=(n // tm, d_in // tn),
        in_specs=[
            pl.BlockSpec((tm, d), lambda i, j: (i, 0)),
            pl.BlockSpec((None, 1, d), lambda i, j: (i // per_b, 0, 0)),
            pl.BlockSpec((None, 1, d), lambda i, j: (i // per_b, 0, 0)),
            pl.BlockSpec((d, tn), lambda i, j: (0, j)),
        ],
        out_specs=pl.BlockSpec((tm, tn), lambda i, j: (i, j)),
        out_shape=jax.ShapeDtypeStruct((n, d_in), F32),
        scratch_shapes=[pltpu.VMEM((tm, d), BF16)],
        compiler_params=_params(("arbitrary", "arbitrary")),
        name="mix_in",
    )(x2, shift, scale, w_in)


def _block_diag_dot(v_bf16, w_ref, n_blocks):
    bd = w_ref.shape[-1]
    outs = [jnp.dot(v_bf16[:, h * bd:(h + 1) * bd], w_ref[h], preferred_element_type=F32)
            for h in range(n_blocks)]
    return jnp.concatenate(outs, axis=1)


def _neg_expm1_nonpos(v, exp_v):
    poly = 1.0 + v * (1.0 / 8.0)
    for k in (7.0, 6.0, 5.0, 4.0, 3.0, 2.0):
        poly = 1.0 + (v * (1.0 / k)) * poly
    return jnp.where(v > -0.1, -v * poly, 1.0 - exp_v)


def _mix_seq_kernel(xl_ref, gl_ref, xp_ref, cw_ref, cb_ref, wa_ref, ba_ref, wx_ref, bx_ref,
                    lam_ref, pw_ref, pb_ref, ps_ref, y_ref, conv_hist, pool_hist, h_carry, *, ts):
    t = pl.program_id(1)

    @pl.when(t == 0)
    def _():
        conv_hist[...] = jnp.zeros_like(conv_hist)
        pool_hist[...] = jnp.zeros_like(pool_hist)
        h_carry[...] = jnp.zeros_like(h_carry)

    d_lru = xl_ref.shape[1]
    row = lax.broadcasted_iota(jnp.int32, (ts, 1), 0)

    xl = xl_ref[...]
    buf = jnp.concatenate([conv_hist[...], xl], axis=0)
    cw = cw_ref[...]
    xc = cb_ref[...] + cw[CONV_WIDTH - 1:CONV_WIDTH] * xl
    for k in range(CONV_WIDTH - 1):
        back = CONV_WIDTH - 1 - k
        xc = xc + cw[k:k + 1] * pltpu.roll(buf, back, 0)[SUBLANES:]
    conv_hist[...] = xl[ts - SUBLANES:]

    xcb = xc.astype(BF16)
    gate_a = jax.nn.sigmoid(_block_diag_dot(xcb, wa_ref, LRU_HEADS) + ba_ref[...])
    gate_x = jax.nn.sigmoid(_block_diag_dot(xcb, wx_ref, LRU_HEADS) + bx_ref[...])
    log_a = (-LRU_C) * gate_a * jax.nn.softplus(-lam_ref[...])
    a = jnp.exp(log_a)
    mult = jnp.sqrt(_neg_expm1_nonpos(2.0 * log_a, a * a))
    u = (xc * gate_x) * mult

    d = 1
    while d < ts:
        keep = row >= d
        a_prev = jnp.where(keep, pltpu.roll(a, d, 0), 1.0)
        u_prev = jnp.where(keep, pltpu.roll(u, d, 0), 0.0)
        u = u + a * u_prev
        a = a * a_prev
        d *= 2
    hs = a * h_carry[0:1, :] + u
    h_carry[...] = jnp.broadcast_to(hs[ts - 1:ts, :], h_carry.shape)
    y_ref[:, :d_lru] = (hs * jax.nn.gelu(gl_ref[...])).astype(y_ref.dtype)

    xp = xp_ref[...]
    bufp = jnp.concatenate([pool_hist[...], xp], axis=0)
    pool_hist[...] = xp[ts - POOL_HIST:]
    t_glob = t * ts + row + 1
    gd = pw_ref.shape[-1]
    ps = []
    for gi, w in enumerate(POOL_WINDOWS):
        s = bufp[:, gi * gd:(gi + 1) * gd]
        sh = 1
        while sh < w:
            s = s + pltpu.roll(s, sh, 0)
            sh *= 2
        cnt = jnp.minimum(t_glob, w).astype(F32)
        ps.append(s[POOL_HIST:] / cnt - xp[:, gi * gd:(gi + 1) * gd])
    p = jnp.concatenate(ps, axis=1).astype(BF16)
    yp = (_block_diag_dot(p, pw_ref, len(POOL_WINDOWS)) + pb_ref[...]) * ps_ref[...]
    y_ref[:, d_lru:] = yp.astype(y_ref.dtype)


def _mix_seq(z, conv_w, conv_b, wa, ba, wx, bx, lam, pool_w, pool_b, pool_scale, nb, seq):
    n = z.shape[0]
    d_lru = conv_w.shape[1]
    d_pool = pool_scale.shape[0]
    assert d_lru == d_pool
    ts = 256
    nt = seq // ts
    row_vec = lambda v: v.reshape(1, -1)
    col_spec = lambda c: pl.BlockSpec((ts, d_lru), lambda b, t: (b * nt + t, c))
    full = lambda a: pl.BlockSpec(a.shape, lambda b, t: (0,) * a.ndim)
    small = [conv_w, row_vec(conv_b), wa.astype(BF16), row_vec(ba), wx.astype(BF16), row_vec(bx),
             row_vec(lam), pool_w.astype(BF16), row_vec(pool_b.reshape(-1)), row_vec(pool_scale)]
    return pl.pallas_call(
        functools.partial(_mix_seq_kernel, ts=ts),
        grid=(nb, nt),
        in_specs=[col_spec(0), col_spec(1), col_spec(2)] + [full(a) for a in small],
        out_specs=pl.BlockSpec((ts, d_lru + d_pool), lambda b, t: (b * nt + t, 0)),
        out_shape=jax.ShapeDtypeStruct((n, d_lru + d_pool), BF16),
        scratch_shapes=[pltpu.VMEM((SUBLANES, d_lru), F32),
                        pltpu.VMEM((POOL_HIST, d_pool), F32),
                        pltpu.VMEM((SUBLANES, d_lru), F32)],
        compiler_params=_params(("arbitrary", "arbitrary")),
        name="mix_seq",
    )(z, z, z, *small)


def _resid_ln(x, o, gate, g, b, alpha):
    v = alpha * x + (1.0 + gate) * o
    mu = jnp.mean(v, axis=-1, keepdims=True)
    dv = v - mu
    var = jnp.mean(dv * dv, axis=-1, keepdims=True)
    return dv * lax.rsqrt(var + LN_EPS) * g + b


def _route(h, wr_ref):
    logits = jnp.dot(h, wr_ref[...], preferred_element_type=F32, precision=lax.Precision.HIGHEST)
    lane = lax.broadcasted_iota(jnp.int32, logits.shape, 1).astype(F32)
    neg = jnp.float32(-jnp.inf)
    lg = jnp.where(lane < N_EXPERTS, logits, neg)
    m1 = jnp.max(lg, axis=-1, keepdims=True)
    i1 = jnp.min(jnp.where(lg == m1, lane, float(LANES)), axis=-1, keepdims=True)
    lg2 = jnp.where(lane == i1, neg, lg)
    m2 = jnp.max(lg2, axis=-1, keepdims=True)
    i2 = jnp.min(jnp.where(lg2 == m2, lane, float(LANES)), axis=-1, keepdims=True)
    e = jnp.exp(m2 - m1)
    p1 = 1.0 / (1.0 + e)
    p2 = e / (1.0 + e)
    idx = jnp.where(lane == 0, i1, i2).astype(jnp.int32)
    return idx, jnp.where(lane == 0, p1, p2)


def _mix_out_kernel(y_ref, x_ref, gate_ref, g_ref, b_ref, shift_ref, scale_ref, w_ref, *rest,
                    alpha, route):
    if route:
        wr_ref, xo_ref, h_ref, idx_ref, p_ref, w_scr = rest
    else:
        xo_ref, h_ref, w_scr = rest

    @pl.when(pl.program_id(0) == 0)
    def _():
        w_scr[...] = w_ref[...].astype(BF16)

    o = jnp.dot(y_ref[...], w_scr[...], preferred_element_type=F32)
    xn = _resid_ln(x_ref[...], o, gate_ref[...], g_ref[...], b_ref[...], alpha)
    xo_ref[...] = xn
    h = xn * (1.0 + scale_ref[...]) + shift_ref[...]
    h_ref[...] = h.astype(h_ref.dtype)
    if route:
        idx, p = _route(h, wr_ref)
        idx_ref[...] = idx
        p_ref[...] = p


def _mix_out(y, x2, gate, ln_g, ln_b, shift2, scale2, w_out, router_w, seq, alpha):
    n, d = x2.shape
    route = router_w is not None
    tm = 256
    per_b = seq // tm
    tok = lambda: pl.BlockSpec((tm, d), lambda i: (i, 0))
    mod = lambda: pl.BlockSpec((None, 1, d), lambda i: (i // per_b, 0, 0))
    vec = lambda: pl.BlockSpec((1, d), lambda i: (0, 0))
    in_specs = [tok(), tok(), mod(), vec(), vec(), mod(), mod(),
                pl.BlockSpec((d, d), lambda i: (0, 0), pipeline_mode=pl.Buffered(1))]
    args = [y, x2, gate, ln_g.reshape(1, d), ln_b.reshape(1, d), shift2, scale2, w_out]
    out_specs = [tok(), tok()]
    out_shape = [jax.ShapeDtypeStruct((n, d), F32),
                 jax.ShapeDtypeStruct((n, d), F32 if route else BF16)]
    if route:
        wr = jnp.pad(router_w, ((0, 0), (0, LANES - router_w.shape[1])))
        in_specs.append(pl.BlockSpec((d, LANES), lambda i: (0, 0)))
        args.append(wr)
        out_specs += [pl.BlockSpec((tm, LANES), lambda i: (i, 0))] * 2
        out_shape += [jax.ShapeDtypeStruct((n, LANES), jnp.int32),
                      jax.ShapeDtypeStruct((n, LANES), F32)]
    return pl.pallas_call(
        functools.partial(_mix_out_kernel, alpha=alpha, route=route),
        grid=(n // tm,),
        in_specs=in_specs,
        out_specs=out_specs,
        out_shape=out_shape,
        scratch_shapes=[pltpu.VMEM((d, d), BF16)],
        compiler_params=_params(("arbitrary",)),
        name="mix_out",
    )(*args)


def _ffn_kernel(ge_ref, gc_ref, gb_ref, x_ref, wg_ref, wu_ref, wd_ref, *rest, ch, alpha, ln):
    if ln:
        xres_hbm, gate_ref, g_ref, b_ref, o_ref, wg_s, wu_s, wd_s, xr_buf, xr_sem = rest
    else:
        o_ref, wg_s, wu_s, wd_s = rest
    s = pl.program_id(0)
    j = pl.program_id(1)
    nj = pl.num_programs(1)
    n = gc_ref[s]
    cap = x_ref.shape[0]

    @pl.when(j == 0)
    def _():
        o_ref[...] = jnp.zeros_like(o_ref)

    def up(c, wg, wu):
        x = x_ref[pl.ds(pl.multiple_of(c * ch, ch), ch), :]
        g = jnp.dot(x, wg, preferred_element_type=F32)
        u = jnp.dot(x, wu, preferred_element_type=F32)
        return (g * jax.nn.sigmoid(g) * u).astype(BF16)

    def down(c, hmid, wd):
        o_ref[pl.ds(pl.multiple_of(c * ch, ch), ch), :] += jnp.dot(
            hmid, wd, preferred_element_type=F32)

    @pl.when(n > 0)
    def _():
        wg = wg_ref[...].astype(BF16)
        wu = wu_ref[...].astype(BF16)
        wd = wd_ref[...].astype(BF16)
        wg_s[...] = wg
        wu_s[...] = wu
        wd_s[...] = wd
        hmid = up(0, wg, wu)

        def body(c, hmid):
            down(c - 1, hmid, wd_s[...])
            return up(c, wg_s[...], wu_s[...])

        hmid = lax.fori_loop(1, n, body, hmid)
        down(n - 1, hmid, wd_s[...])

    if ln:
        row0 = gb_ref[s] * cap

        def xres_copy(c, slot):
            return pltpu.make_async_copy(
                xres_hbm.at[pl.ds(row0 + c * ch, ch)], xr_buf.at[slot], xr_sem.at[slot])

        @pl.when(jnp.logical_and(j == nj - 1, n > 0))
        def _():
            xres_copy(0, 0).start()

            def fin(c, carry):
                slot = c % 2

                @pl.when(c + 1 < n)
                def _():
                    xres_copy(c + 1, 1 - slot).start()

                xres_copy(c, slot).wait()
                rows = pl.ds(pl.multiple_of(c * ch, ch), ch)
                o_ref[rows, :] = _resid_ln(xr_buf[slot], o_ref[rows, :],
                                           gate_ref[...], g_ref[...], b_ref[...], alpha)
                return carry

            lax.fori_loop(0, n, fin, 0)


def _ffn(x_rows, w_gate, w_up, w_down, group_expert, group_chunks, group_block, *, cap, tf, ch,
         x_buffers, ln_args=None, seq=None, alpha=None):
    p_rows, d = x_rows.shape
    f = w_gate.shape[-1]
    n_groups = group_expert.shape[0]
    nj = f // tf
    ln = ln_args is not None

    def jj(s, j, gc):
        return jnp.where(gc[s] > 0, j, nj - 1)

    in_specs = [
        pl.BlockSpec((cap, d), lambda s, j, ge, gc, gb: (gb[s], 0),
                     pipeline_mode=pl.Buffered(x_buffers)),
        pl.BlockSpec((None, d, tf), lambda s, j, ge, gc, gb: (ge[s], 0, jj(s, j, gc))),
        pl.BlockSpec((None, d, tf), lambda s, j, ge, gc, gb: (ge[s], 0, jj(s, j, gc))),
        pl.BlockSpec((None, tf, d), lambda s, j, ge, gc, gb: (ge[s], jj(s, j, gc), 0)),
    ]
    args = [x_rows, w_gate, w_up, w_down]
    scratch = [pltpu.VMEM((d, tf), BF16), pltpu.VMEM((d, tf), BF16), pltpu.VMEM((tf, d), BF16)]
    if ln:
        xres, gate, ln_g, ln_b = ln_args
        assert seq % cap == 0
        per_b = seq // cap
        in_specs += [
            pl.BlockSpec(memory_space=pl.ANY),
            pl.BlockSpec((None, 1, d), lambda s, j, ge, gc, gb: (gb[s] // per_b, 0, 0)),
            pl.BlockSpec((1, d), lambda s, j, ge, gc, gb: (0, 0)),
            pl.BlockSpec((1, d), lambda s, j, ge, gc, gb: (0, 0)),
        ]
        args += [xres, gate, ln_g.reshape(1, d), ln_b.reshape(1, d)]
        scratch += [pltpu.VMEM((2, ch, d), F32), pltpu.SemaphoreType.DMA((2,))]
    grid_spec = pltpu.PrefetchScalarGridSpec(
        num_scalar_prefetch=3,
        grid=(n_groups, nj),
        in_specs=in_specs,
        out_specs=pl.BlockSpec((cap, d), lambda s, j, ge, gc, gb: (s, 0),
                               pipeline_mode=pl.Buffered(1)),
        scratch_shapes=scratch,
    )
    return pl.pallas_call(
        functools.partial(_ffn_kernel, ch=ch, alpha=alpha, ln=ln),
        grid_spec=grid_spec,
        out_shape=jax.ShapeDtypeStruct((p_rows, d), F32),
        compiler_params=_params(("arbitrary", "arbitrary")),
        name="ffn_ln" if ln else "ffn_moe",
    )(group_expert, group_chunks, group_block, *args)


def _row_copy(src_hbm, src_row, dst_buf, dst_row, sem):
    return pltpu.make_async_copy(src_hbm.at[pl.ds(src_row, 1)], dst_buf.at[pl.ds(dst_row, 1)], sem)


def _gather_kernel(tok_ref, used_ref, h_hbm, o_ref, buf, sem, *, tg):
    i = pl.program_id(0)
    n = pl.num_programs(0)

    def issue(tile, slot):
        def body(r, carry):
            _row_copy(h_hbm, tok_ref[tile * tg + r], buf.at[slot], r, sem.at[slot]).start()
            return carry
        lax.fori_loop(0, tg, body, 0, unroll=8)

    def drain(slot):
        def body(r, carry):
            _row_copy(h_hbm, 0, buf.at[slot], r, sem.at[slot]).wait()
            return carry
        lax.fori_loop(0, tg, body, 0, unroll=8)

    def live(tile):
        return tile * tg < used_ref[0]

    @pl.when(jnp.logical_and(i == 0, live(0)))
    def _():
        issue(0, 0)

    @pl.when(jnp.logical_and(i + 1 < n, live(i + 1)))
    def _():
        issue(i + 1, (i + 1) % 2)

    @pl.when(live(i))
    def _():
        drain(i % 2)
        o_ref[...] = buf[i % 2].astype(o_ref.dtype)

    @pl.when(jnp.logical_not(live(i)))
    def _():
        o_ref[...] = jnp.zeros_like(o_ref)


def _gather(h, tok_of_row, used_rows, p_rows):
    n, d = h.shape
    tg = 256
    grid_spec = pltpu.PrefetchScalarGridSpec(
        num_scalar_prefetch=2,
        grid=(p_rows // tg,),
        in_specs=[pl.BlockSpec(memory_space=pl.ANY)],
        out_specs=pl.BlockSpec((tg, d), lambda i, tok, used: (i, 0)),
        scratch_shapes=[pltpu.VMEM((2, tg, d), F32), pltpu.SemaphoreType.DMA((2,))],
    )
    return pl.pallas_call(
        functools.partial(_gather_kernel, tg=tg),
        grid_spec=grid_spec,
        out_shape=jax.ShapeDtypeStruct((p_rows, d), BF16),
        compiler_params=_params(("arbitrary",)),
        name="gather",
    )(tok_of_row, used_rows, h)


def _combine_kernel(dest_ref, x_ref, p_ref, gate_ref, g_ref, b_ref, ys_hbm, o_ref, buf, sem,
                    *, tc, alpha):
    i = pl.program_id(0)
    n = pl.num_programs(0)

    def issue(tile, slot):
        def body(r, carry):
            for k in range(TOP_K):
                src = dest_ref[(tile * tc + r) * TOP_K + k]
                _row_copy(ys_hbm, src, buf.at[slot, k], r, sem.at[slot]).start()
            return carry
        lax.fori_loop(0, tc, body, 0, unroll=4)

    def drain(slot):
        def body(r, carry):
            for k in range(TOP_K):
                _row_copy(ys_hbm, 0, buf.at[slot, k], r, sem.at[slot]).wait()
            return carry
        lax.fori_loop(0, tc, body, 0, unroll=4)

    @pl.when(i == 0)
    def _():
        issue(0, 0)

    @pl.when(i + 1 < n)
    def _():
        issue(i + 1, (i + 1) % 2)

    slot = i % 2
    drain(slot)
    p = p_ref[...]
    o = p[:, 0:1] * buf[slot, 0] + p[:, 1:2] * buf[slot, 1]
    o_ref[...] = _resid_ln(x_ref[...], o, gate_ref[...], g_ref[...], b_ref[...], alpha)


def _combine(ys, dest, x2, probs, gate, ln_g, ln_b, seq, alpha):
    n, d = x2.shape
    tc = 256
    per_b = seq // tc
    grid_spec = pltpu.PrefetchScalarGridSpec(
        num_scalar_prefetch=1,
        grid=(n // tc,),
        in_specs=[
            pl.BlockSpec((tc, d), lambda i, dst: (i, 0)),
            pl.BlockSpec((tc, LANES), lambda i, dst: (i, 0)),
            pl.BlockSpec((None, 1, d), lambda i, dst: (i // per_b, 0, 0)),
            pl.BlockSpec((1, d), lambda i, dst: (0, 0)),
            pl.BlockSpec((1, d), lambda i, dst: (0, 0)),
            pl.BlockSpec(memory_space=pl.ANY),
        ],
        out_specs=pl.BlockSpec((tc, d), lambda i, dst: (i, 0)),
        scratch_shapes=[pltpu.VMEM((2, TOP_K, tc, d), F32), pltpu.SemaphoreType.DMA((2,))],
    )
    return pl.pallas_call(
        functools.partial(_combine_kernel, tc=tc, alpha=alpha),
        grid_spec=grid_spec,
        out_shape=jax.ShapeDtypeStruct((n, d), F32),
        compiler_params=_params(("arbitrary",)),
        name="combine",
    )(dest, x2, probs, gate, ln_g.reshape(1, d), ln_b.reshape(1, d), ys)


def _routing_tables(top_i, cap, ch, max_groups):
    n = top_i.shape[0]
    e_flat = top_i.reshape(-1)
    onehot = (e_flat[:, None] == jnp.arange(N_EXPERTS, dtype=jnp.int32)[None, :]).astype(jnp.int32)
    csum = jnp.cumsum(onehot, axis=0)
    counts = csum[-1]
    rank = jnp.take_along_axis(csum, e_flat[:, None], axis=1)[:, 0] - 1
    groups_per = (counts + cap - 1) // cap
    group_end = jnp.cumsum(groups_per)
    group_start = group_end - groups_per
    dest = (group_start * cap)[e_flat] + rank
    n_groups = group_end[-1]
    gidx = jnp.arange(max_groups, dtype=jnp.int32)
    gclamp = jnp.minimum(gidx, n_groups - 1)
    group_expert = jnp.sum((group_end[None, :] <= gclamp[:, None]).astype(jnp.int32), axis=1)
    rows = jnp.clip(counts[group_expert] - (gclamp - group_start[group_expert]) * cap, 0, cap)
    group_chunks = jnp.where(gidx < n_groups, (rows + ch - 1) // ch, 0)
    tok_of_row = jnp.zeros((max_groups * cap,), jnp.int32).at[dest].set(
        jnp.arange(n * TOP_K, dtype=jnp.int32) // TOP_K)
    used_rows = (n_groups * cap).reshape(1).astype(jnp.int32)
    return (dest.astype(jnp.int32), tok_of_row, used_rows,
            group_expert.astype(jnp.int32), group_chunks.astype(jnp.int32), gclamp.astype(jnp.int32))


def kernel(x, c, ada_w, ada_b, ln_g, ln_b, mix_w_in, conv_w, conv_b, lru_wa, lru_ba, lru_wx, lru_bx,
           lru_lam, pool_w, pool_b, pool_scale, mix_w_out, ffn_w_gate, ffn_w_up, ffn_w_down,
           router_w, exp_w_gate, exp_w_up, exp_w_down):
    nb, seq, d = x.shape
    depth = ada_w.shape[0]
    n = nb * seq
    alpha = float((2 * depth) ** 0.25)
    mods = _ada(c, ada_w, ada_b)
    x2 = x.reshape(n, d)

    ffn_tf, ffn_ch = 256, 256
    dense_cap = seq
    moe_cap = 2304
    for l in range(depth):
        shift1, scale1, gate1 = mods[l, 0, :, 0], mods[l, 0, :, 1], mods[l, 0, :, 2]
        shift2, scale2, gate2 = mods[l, 1, :, 0], mods[l, 1, :, 1], mods[l, 1, :, 2]
        moe = (l % 2 == 1)
        i = l // 2

        z = _mix_in(x2, shift1, scale1, mix_w_in[l], seq)
        y = _mix_seq(z, conv_w[l], conv_b[l], lru_wa[l], lru_ba[l], lru_wx[l], lru_bx[l],
                     lru_lam[l], pool_w[l], pool_b[l], pool_scale[l], nb, seq)
        outs = _mix_out(y, x2, gate1, ln_g[l, 0], ln_b[l, 0], shift2, scale2, mix_w_out[l],
                        router_w[i] if moe else None, seq, alpha)
        if not moe:
            x2, h = outs
            n_groups = n // dense_cap
            ge = jnp.full((n_groups,), i, jnp.int32)
            gc = jnp.full((n_groups,), dense_cap // ffn_ch, jnp.int32)
            gb = jnp.arange(n_groups, dtype=jnp.int32)
            x2 = _ffn(h, ffn_w_gate, ffn_w_up, ffn_w_down, ge, gc, gb,
                      cap=dense_cap, tf=ffn_tf, ch=ffn_ch, x_buffers=2,
                      ln_args=(x2, gate2, ln_g[l, 1], ln_b[l, 1]), seq=seq, alpha=alpha)
        else:
            x2, h, idx, probs = outs
            max_groups = (n * TOP_K) // moe_cap + N_EXPERTS
            dest, tok_of_row, used_rows, ge, gc, gb = _routing_tables(
                idx[:, :TOP_K], moe_cap, ffn_ch, max_groups)
            xs = _gather(h, tok_of_row, used_rows, max_groups * moe_cap)
            n_exp = exp_w_gate.shape[1]
            f_exp = exp_w_gate.shape[-1]
            ys = _ffn(xs,
                      exp_w_gate.reshape(-1, d, f_exp), exp_w_up.reshape(-1, d, f_exp),
                      exp_w_down.reshape(-1, f_exp, d), ge + i * n_exp, gc, gb,
                      cap=moe_cap, tf=ffn_tf, ch=ffn_ch, x_buffers=1)
            x2 = _combine(ys, dest, x2, probs, gate2, ln_g[l, 1], ln_b[l, 1], seq, alpha)
    return x2.reshape(nb, seq, d)
```

```python
import functools

import jax
import jax.numpy as jnp
from jax import lax
from jax.experimental import pallas as pl
from jax.experimental.pallas import tpu as pltpu

F32 = jnp.float32
BF16 = jnp.bfloat16

LN_EPS = 1e-5
LRU_C = 8.0
CONV_WIDTH = 4
LRU_HEADS = 4
POOL_WINDOWS = (2, 4, 8, 16)
N_EXPERTS = 8
TOP_K = 2

V7X_VMEM_LIMIT_BYTES = 58 * 1024 * 1024
LANES = 128
SUBLANES = 8
POOL_HIST = 16
SHIFT, SCALE, GATE = 0, 1, 2


def _params(sem, vmem=V7X_VMEM_LIMIT_BYTES):
    return pltpu.CompilerParams(dimension_semantics=sem, vmem_limit_bytes=vmem)


def _mod_spec(mods, l, k, which, batch_of):
    d = mods.shape[-1]
    return pl.BlockSpec((None, None, None, None, 1, d),
                        lambda *g: (l, k, batch_of(*g), which, 0, 0))


def _layer_spec(arr, l):
    nd = arr.ndim - 1
    return pl.BlockSpec((None,) + arr.shape[1:], lambda *g: (l,) + (0,) * nd)


def _ada_kernel(c_ref, w_ref, b_ref, o_ref):
    c = c_ref[...]
    c_act = (c * jax.nn.sigmoid(c)).astype(BF16)
    o_ref[...] = jnp.dot(c_act, w_ref[...].astype(BF16),
                         preferred_element_type=F32) + b_ref[...]


def _ada(c, ada_w, ada_b):
    depth, _, d, d3 = ada_w.shape
    nb = c.shape[0]
    rows = -(-nb // SUBLANES) * SUBLANES
    c_pad = jnp.pad(c, ((0, rows - nb), (0, 0)))
    w = ada_w.reshape(depth * 2, d, d3)
    b = ada_b.reshape(depth * 2, 1, d3)
    tn = 1024
    out = pl.pallas_call(
        _ada_kernel,
        grid=(depth * 2, d3 // tn),
        in_specs=[
            pl.BlockSpec((rows, d), lambda l, j: (0, 0)),
            pl.BlockSpec((None, d, tn), lambda l, j: (l, 0, j)),
            pl.BlockSpec((None, 1, tn), lambda l, j: (l, 0, j)),
        ],
        out_specs=pl.BlockSpec((None, rows, tn), lambda l, j: (l, 0, j)),
        out_shape=jax.ShapeDtypeStruct((depth * 2, rows, d3), F32),
        compiler_params=_params(("arbitrary", "arbitrary")),
        name="ada",
    )(c_pad, w, b)
    return out.reshape(depth, 2, rows, 3, 1, d)


def _mix_in_kernel(x_ref, shift_ref, scale_ref, w_ref, z_ref, h_scr):
    @pl.when(pl.program_id(1) == 0)
    def _():
        h_scr[...] = (x_ref[...] * (1.0 + scale_ref[...]) + shift_ref[...]).astype(BF16)

    z_ref[...] = jnp.dot(h_scr[...], w_ref[...].astype(BF16), preferred_element_type=F32)


def _mix_in(x2, mods, l, w_in_all, seq):
    n, d = x2.shape
    d_in = w_in_all.shape[-1]
    tm, tn = 1024, 1024
    per_b = seq // tm
    batch_of = lambda i, j: i // per_b
    return pl.pallas_call(
        _mix_in_kernel,
        grid=(n // tm, d_in // tn),
        in_specs=[
            pl.BlockSpec((tm, d), lambda i, j: (i, 0)),
            _mod_spec(mods, l, 0, SHIFT, batch_of),
            _mod_spec(mods, l, 0, SCALE, batch_of),
            pl.BlockSpec((None, d, tn), lambda i, j: (l, 0, j)),
        ],
        out_specs=pl.BlockSpec((tm, tn), lambda i, j: (i, j)),
        out_shape=jax.ShapeDtypeStruct((n, d_in), F32),
        scratch_shapes=[pltpu.VMEM((tm, d), BF16)],
        compiler_params=_params(("arbitrary", "arbitrary")),
        name="mix_in",
    )(x2, mods, mods, w_in_all)


def _block_diag_dot(v_bf16, w_ref, n_blocks):
    bd = w_ref.shape[-1]
    outs = [jnp.dot(v_bf16[:, h * bd:(h + 1) * bd], w_ref[h], preferred_element_type=F32)
            for h in range(n_blocks)]
    return jnp.concatenate(outs, axis=1)


def _neg_expm1_nonpos(v, exp_v):
    poly = 1.0 + v * (1.0 / 8.0)
    for k in (7.0, 6.0, 5.0, 4.0, 3.0, 2.0):
        poly = 1.0 + (v * (1.0 / k)) * poly
    return jnp.where(v > -0.1, -v * poly, 1.0 - exp_v)


def _mix_seq_kernel(xl_ref, gl_ref, xp_ref, cw_ref, cb_ref, wa_ref, ba_ref, wx_ref, bx_ref,
                    lam_ref, pw_ref, pb_ref, ps_ref, y_ref, conv_hist, pool_hist, h_carry, *, ts):
    t = pl.program_id(1)

    @pl.when(t == 0)
    def _():
        conv_hist[...] = jnp.zeros_like(conv_hist)
        pool_hist[...] = jnp.zeros_like(pool_hist)
        h_carry[...] = jnp.zeros_like(h_carry)

    d_lru = xl_ref.shape[1]
    row = lax.broadcasted_iota(jnp.int32, (ts, 1), 0)

    xl = xl_ref[...]
    buf = jnp.concatenate([conv_hist[...], xl], axis=0)
    cw = cw_ref[...]
    xc = cb_ref[...] + cw[CONV_WIDTH - 1:CONV_WIDTH] * xl
    for k in range(CONV_WIDTH - 1):
        back = CONV_WIDTH - 1 - k
        xc = xc + cw[k:k + 1] * pltpu.roll(buf, back, 0)[SUBLANES:]
    conv_hist[...] = xl[ts - SUBLANES:]

    xcb = xc.astype(BF16)
    gate_a = jax.nn.sigmoid(_block_diag_dot(xcb, wa_ref, LRU_HEADS) + ba_ref[...])
    gate_x = jax.nn.sigmoid(_block_diag_dot(xcb, wx_ref, LRU_HEADS) + bx_ref[...])
    log_a = (-LRU_C) * gate_a * jax.nn.softplus(-lam_ref[...])
    a = jnp.exp(log_a)
    mult = jnp.sqrt(_neg_expm1_nonpos(2.0 * log_a, a * a))
    u = (xc * gate_x) * mult

    d = 1
    while d < ts:
        keep = row >= d
        a_prev = jnp.where(keep, pltpu.roll(a, d, 0), 1.0)
        u_prev = jnp.where(keep, pltpu.roll(u, d, 0), 0.0)
        u = u + a * u_prev
        a = a * a_prev
        d *= 2
    hs = a * h_carry[0:1, :] + u
    h_carry[...] = jnp.broadcast_to(hs[ts - 1:ts, :], h_carry.shape)
    y_ref[:, :d_lru] = (hs * jax.nn.gelu(gl_ref[...])).astype(y_ref.dtype)

    xp = xp_ref[...]
    bufp = jnp.concatenate([pool_hist[...], xp], axis=0)
    pool_hist[...] = xp[ts - POOL_HIST:]
    t_glob = t * ts + row + 1
    gd = pw_ref.shape[-1]
    ps = []
    for gi, w in enumerate(POOL_WINDOWS):
        s = bufp[:, gi * gd:(gi + 1) * gd]
        sh = 1
        while sh < w:
            s = s + pltpu.roll(s, sh, 0)
            sh *= 2
        cnt = jnp.minimum(t_glob, w).astype(F32)
        ps.append(s[POOL_HIST:] / cnt - xp[:, gi * gd:(gi + 1) * gd])
    p = jnp.concatenate(ps, axis=1).astype(BF16)
    yp = (_block_diag_dot(p, pw_ref, len(POOL_WINDOWS)) + pb_ref[...]) * ps_ref[...]
    y_ref[:, d_lru:] = yp.astype(y_ref.dtype)


def _mix_seq(z, l, small, nb, seq):
    n = z.shape[0]
    d_lru = small[0].shape[-1]
    ts = 256
    nt = seq // ts
    col_spec = lambda c: pl.BlockSpec((ts, d_lru), lambda b, t: (b * nt + t, c))
    return pl.pallas_call(
        functools.partial(_mix_seq_kernel, ts=ts),
        grid=(nb, nt),
        in_specs=[col_spec(0), col_spec(1), col_spec(2)] + [_layer_spec(a, l) for a in small],
        out_specs=pl.BlockSpec((ts, 2 * d_lru), lambda b, t: (b * nt + t, 0)),
        out_shape=jax.ShapeDtypeStruct((n, 2 * d_lru), BF16),
        scratch_shapes=[pltpu.VMEM((SUBLANES, d_lru), F32),
                        pltpu.VMEM((POOL_HIST, d_lru), F32),
                        pltpu.VMEM((SUBLANES, d_lru), F32)],
        compiler_params=_params(("arbitrary", "arbitrary")),
        name="mix_seq",
    )(z, z, z, *small)


def _resid_ln(x, o, gate, g, b, alpha):
    v = alpha * x + (1.0 + gate) * o
    mu = jnp.mean(v, axis=-1, keepdims=True)
    dv = v - mu
    var = jnp.mean(dv * dv, axis=-1, keepdims=True)
    return dv * lax.rsqrt(var + LN_EPS) * g + b


def _route(h, wr_ref):
    tm = h.shape[0]
    h_hi = h.astype(BF16)
    h_lo = (h - h_hi.astype(F32)).astype(BF16)
    r = jnp.dot(jnp.concatenate([h_hi, h_lo], axis=0), wr_ref[...], preferred_element_type=F32)
    r = r[:tm] + r[tm:]
    logits = r + pltpu.roll(r, LANES - N_EXPERTS, 1)
    lane = lax.broadcasted_iota(jnp.int32, logits.shape, 1).astype(F32)
    neg = jnp.float32(-jnp.inf)
    lg = jnp.where(lane < N_EXPERTS, logits, neg)
    m1 = jnp.max(lg, axis=-1, keepdims=True)
    i1 = jnp.min(jnp.where(lg == m1, lane, float(LANES)), axis=-1, keepdims=True)
    lg2 = jnp.where(lane == i1, neg, lg)
    m2 = jnp.max(lg2, axis=-1, keepdims=True)
    i2 = jnp.min(jnp.where(lg2 == m2, lane, float(LANES)), axis=-1, keepdims=True)
    e = jnp.exp(m2 - m1)
    p1 = 1.0 / (1.0 + e)
    p2 = e / (1.0 + e)
    idx = jnp.where(lane == 0, i1, i2).astype(jnp.int32)
    return idx, jnp.where(lane == 0, p1, p2)


def _router_weights(router_w):
    hi = router_w.astype(BF16)
    lo = (router_w - hi.astype(F32)).astype(BF16)
    wr = jnp.concatenate([hi, lo], axis=1)
    return jnp.pad(wr, ((0, 0), (0, LANES - wr.shape[1])))


def _mix_out_kernel(y_ref, x_ref, gate_ref, g_ref, b_ref, shift_ref, scale_ref, w_ref, *rest,
                    alpha, route):
    if route:
        wr_ref, xo_ref, h_ref, idx_ref, p_ref, w_scr = rest
    else:
        xo_ref, h_ref, w_scr = rest

    @pl.when(pl.program_id(0) == 0)
    def _():
        w_scr[...] = w_ref[...].astype(BF16)

    o = jnp.dot(y_ref[...], w_scr[...], preferred_element_type=F32)
    xn = _resid_ln(x_ref[...], o, gate_ref[...], g_ref[...], b_ref[...], alpha)
    xo_ref[...] = xn
    h = xn * (1.0 + scale_ref[...]) + shift_ref[...]
    h_ref[...] = h.astype(h_ref.dtype)
    if route:
        idx, p = _route(h, wr_ref)
        idx_ref[...] = idx
        p_ref[...] = p


def _mix_out(y, x2, mods, l, ln_g, ln_b, w_out_all, router_w, seq, alpha):
    n, d = x2.shape
    route = router_w is not None
    tm = 256
    per_b = seq // tm
    batch_of = lambda i: i // per_b
    tok = lambda: pl.BlockSpec((tm, d), lambda i: (i, 0))
    vec = lambda k: pl.BlockSpec((None, None, 1, d), lambda i: (l, k, 0, 0))
    ln_g4 = ln_g.reshape(ln_g.shape[0], 2, 1, d)
    ln_b4 = ln_b.reshape(ln_b.shape[0], 2, 1, d)
    in_specs = [tok(), tok(), _mod_spec(mods, l, 0, GATE, batch_of), vec(0), vec(0),
                _mod_spec(mods, l, 1, SHIFT, batch_of), _mod_spec(mods, l, 1, SCALE, batch_of),
                pl.BlockSpec((None, d, d), lambda i: (l, 0, 0), pipeline_mode=pl.Buffered(1))]
    args = [y, x2, mods, ln_g4, ln_b4, mods, mods, w_out_all]
    out_specs = [tok(), tok()]
    out_shape = [jax.ShapeDtypeStruct((n, d), F32),
                 jax.ShapeDtypeStruct((n, d), F32 if route else BF16)]
    if route:
        in_specs.append(pl.BlockSpec((d, LANES), lambda i: (0, 0)))
        args.append(_router_weights(router_w))
        out_specs += [pl.BlockSpec((tm, LANES), lambda i: (i, 0))] * 2
        out_shape += [jax.ShapeDtypeStruct((n, LANES), jnp.int32),
                      jax.ShapeDtypeStruct((n, LANES), F32)]
    return pl.pallas_call(
        functools.partial(_mix_out_kernel, alpha=alpha, route=route),
        grid=(n // tm,),
        in_specs=in_specs,
        out_specs=out_specs,
        out_shape=out_shape,
        scratch_shapes=[pltpu.VMEM((d, d), BF16)],
        compiler_params=_params(("arbitrary",)),
        name="mix_out",
    )(*args)


def _ffn_kernel(ge_ref, gc_ref, gb_ref, x_ref, wg_ref, wu_ref, wd_ref, *rest, ch, alpha, ln):
    if ln:
        xres_hbm, gate_ref, g_ref, b_ref, o_ref, wg_s, wu_s, wd_s, xr_buf, xr_sem = rest
    else:
        o_ref, wg_s, wu_s, wd_s = rest
    s = pl.program_id(0)
    j = pl.program_id(1)
    nj = pl.num_programs(1)
    n = gc_ref[s]
    cap = x_ref.shape[0]

    @pl.when(j == 0)
    def _():
        o_ref[...] = jnp.zeros_like(o_ref)

    def up(c, wg, wu):
        x = x_ref[pl.ds(pl.multiple_of(c * ch, ch), ch), :]
        g = jnp.dot(x, wg, preferred_element_type=F32)
        u = jnp.dot(x, wu, preferred_element_type=F32)
        return (g * jax.nn.sigmoid(g) * u).astype(BF16)

    def down(c, hmid, wd):
        o_ref[pl.ds(pl.multiple_of(c * ch, ch), ch), :] += jnp.dot(
            hmid, wd, preferred_element_type=F32)

    def fused(c, hmid):
        down(c - 1, hmid, wd_s[...])
        return up(c, wg_s[...], wu_s[...])

    @pl.when(n > 0)
    def _():
        wg = wg_ref[...].astype(BF16)
        wu = wu_ref[...].astype(BF16)
        wd = wd_ref[...].astype(BF16)
        wg_s[...] = wg
        wu_s[...] = wu
        wd_s[...] = wd
        hmid = up(0, wg, wu)

        pairs = lax.div(n - 1, 2)

        def two(p, hmid):
            return fused(2 * p + 2, fused(2 * p + 1, hmid))

        hmid = lax.fori_loop(0, pairs, two, hmid)
        hmid = lax.fori_loop(2 * pairs + 1, n, fused, hmid)
        down(n - 1, hmid, wd_s[...])

    if ln:
        row0 = gb_ref[s] * cap

        def xres_copy(c, slot):
            return pltpu.make_async_copy(
                xres_hbm.at[pl.ds(row0 + c * ch, ch)], xr_buf.at[slot], xr_sem.at[slot])

        @pl.when(jnp.logical_and(j == nj - 1, n > 0))
        def _():
            xres_copy(0, 0).start()

            def fin(c, carry):
                slot = c % 2

                @pl.when(c + 1 < n)
                def _():
                    xres_copy(c + 1, 1 - slot).start()

                xres_copy(c, slot).wait()
                rows = pl.ds(pl.multiple_of(c * ch, ch), ch)
                o_ref[rows, :] = _resid_ln(xr_buf[slot], o_ref[rows, :],
                                           gate_ref[...], g_ref[...], b_ref[...], alpha)
                return carry

            lax.fori_loop(0, n, fin, 0)


def _ffn(x_rows, w_gate, w_up, w_down, group_expert, group_chunks, group_block, *, cap, tf, ch,
         x_buffers, ln_args=None, seq=None, alpha=None):
    p_rows, d = x_rows.shape
    f = w_gate.shape[-1]
    n_groups = group_expert.shape[0]
    nj = f // tf
    ln = ln_args is not None

    def jj(s, j, gc):
        return jnp.where(gc[s] > 0, j, nj - 1)

    in_specs = [
        pl.BlockSpec((cap, d), lambda s, j, ge, gc, gb: (gb[s], 0),
                     pipeline_mode=pl.Buffered(x_buffers)),
        pl.BlockSpec((None, d, tf), lambda s, j, ge, gc, gb: (ge[s], 0, jj(s, j, gc))),
        pl.BlockSpec((None, d, tf), lambda s, j, ge, gc, gb: (ge[s], 0, jj(s, j, gc))),
        pl.BlockSpec((None, tf, d), lambda s, j, ge, gc, gb: (ge[s], jj(s, j, gc), 0)),
    ]
    args = [x_rows, w_gate, w_up, w_down]
    scratch = [pltpu.VMEM((d, tf), BF16), pltpu.VMEM((d, tf), BF16), pltpu.VMEM((tf, d), BF16)]
    if ln:
        xres, mods, l, ln_g, ln_b = ln_args
        assert seq % cap == 0
        per_b = seq // cap
        ln_g4 = ln_g.reshape(ln_g.shape[0], 2, 1, d)
        ln_b4 = ln_b.reshape(ln_b.shape[0], 2, 1, d)
        vec = lambda: pl.BlockSpec((None, None, 1, d), lambda s, j, ge, gc, gb: (l, 1, 0, 0))
        in_specs += [
            pl.BlockSpec(memory_space=pl.ANY),
            _mod_spec(mods, l, 1, GATE, lambda s, j, ge, gc, gb: gb[s] // per_b),
            vec(), vec(),
        ]
        args += [xres, mods, ln_g4, ln_b4]
        scratch += [pltpu.VMEM((2, ch, d), F32), pltpu.SemaphoreType.DMA((2,))]
    grid_spec = pltpu.PrefetchScalarGridSpec(
        num_scalar_prefetch=3,
        grid=(n_groups, nj),
        in_specs=in_specs,
        out_specs=pl.BlockSpec((cap, d), lambda s, j, ge, gc, gb: (s, 0),
                               pipeline_mode=pl.Buffered(1)),
        scratch_shapes=scratch,
    )
    return pl.pallas_call(
        functools.partial(_ffn_kernel, ch=ch, alpha=alpha, ln=ln),
        grid_spec=grid_spec,
        out_shape=jax.ShapeDtypeStruct((p_rows, d), F32),
        compiler_params=_params(("arbitrary", "arbitrary")),
        name="ffn_ln" if ln else "ffn_moe",
    )(group_expert, group_chunks, group_block, *args)


def _row_copy(src_hbm, src_row, dst_buf, dst_row, sem):
    return pltpu.make_async_copy(src_hbm.at[pl.ds(src_row, 1)], dst_buf.at[pl.ds(dst_row, 1)], sem)


def _next_tile_spec(n_tiles, width):
    return pl.BlockSpec((None, 1, width), lambda i, *_: (jnp.minimum(i + 1, n_tiles - 1), 0, 0),
                        memory_space=pltpu.SMEM)


def _first_tile_spec(width):
    return pl.BlockSpec((None, 1, width), lambda i, *_: (0, 0, 0), memory_space=pltpu.SMEM)


def _gather_kernel(used_ref, tok0_ref, tokn_ref, h_hbm, o_ref, buf, sem, *, tg):
    i = pl.program_id(0)
    n = pl.num_programs(0)

    def issue(tok_ref, slot):
        def body(r, carry):
            _row_copy(h_hbm, tok_ref[0, r], buf.at[slot], r, sem.at[slot]).start()
            return carry
        lax.fori_loop(0, tg, body, 0, unroll=8)

    def drain(slot):
        def body(r, carry):
            _row_copy(h_hbm, 0, buf.at[slot], r, sem.at[slot]).wait()
            return carry
        lax.fori_loop(0, tg, body, 0, unroll=8)

    def live(tile):
        return tile * tg < used_ref[0]

    @pl.when(jnp.logical_and(i == 0, live(0)))
    def _():
        issue(tok0_ref, 0)

    @pl.when(jnp.logical_and(i + 1 < n, live(i + 1)))
    def _():
        issue(tokn_ref, (i + 1) % 2)

    @pl.when(live(i))
    def _():
        drain(i % 2)
        o_ref[...] = buf[i % 2].astype(o_ref.dtype)

    @pl.when(jnp.logical_not(live(i)))
    def _():
        o_ref[...] = jnp.zeros_like(o_ref)


def _gather(h, tok_of_row, used_rows, p_rows):
    n, d = h.shape
    tg = 256
    n_tiles = p_rows // tg
    tok3 = tok_of_row.reshape(n_tiles, 1, tg)
    grid_spec = pltpu.PrefetchScalarGridSpec(
        num_scalar_prefetch=1,
        grid=(n_tiles,),
        in_specs=[_first_tile_spec(tg), _next_tile_spec(n_tiles, tg),
                  pl.BlockSpec(memory_space=pl.ANY)],
        out_specs=pl.BlockSpec((tg, d), lambda i, used: (i, 0)),
        scratch_shapes=[pltpu.VMEM((2, tg, d), F32), pltpu.SemaphoreType.DMA((2,))],
    )
    return pl.pallas_call(
        functools.partial(_gather_kernel, tg=tg),
        grid_spec=grid_spec,
        out_shape=jax.ShapeDtypeStruct((p_rows, d), BF16),
        compiler_params=_params(("arbitrary",)),
        name="gather",
    )(used_rows, tok3, tok3, h)


def _combine_kernel(dst0_ref, dstn_ref, x_ref, p_ref, gate_ref, g_ref, b_ref, ys_hbm, o_ref,
                    buf, sem, *, tc, alpha):
    i = pl.program_id(0)
    n = pl.num_programs(0)

    def issue(dst_ref, slot):
        def body(r, carry):
            for k in range(TOP_K):
                _row_copy(ys_hbm, dst_ref[0, r * TOP_K + k], buf.at[slot, k], r,
                          sem.at[slot]).start()
            return carry
        lax.fori_loop(0, tc, body, 0, unroll=4)

    def drain(slot):
        def body(r, carry):
            for k in range(TOP_K):
                _row_copy(ys_hbm, 0, buf.at[slot, k], r, sem.at[slot]).wait()
            return carry
        lax.fori_loop(0, tc, body, 0, unroll=4)

    @pl.when(i == 0)
    def _():
        issue(dst0_ref, 0)

    @pl.when(i + 1 < n)
    def _():
        issue(dstn_ref, (i + 1) % 2)

    slot = i % 2
    drain(slot)
    p = p_ref[...]
    o = p[:, 0:1] * buf[slot, 0] + p[:, 1:2] * buf[slot, 1]
    o_ref[...] = _resid_ln(x_ref[...], o, gate_ref[...], g_ref[...], b_ref[...], alpha)


def _combine(ys, dest, x2, probs, mods, l, ln_g, ln_b, seq, alpha):
    n, d = x2.shape
    tc = 256
    n_tiles = n // tc
    per_b = seq // tc
    dest3 = dest.reshape(n_tiles, 1, tc * TOP_K)
    ln_g4 = ln_g.reshape(ln_g.shape[0], 2, 1, d)
    ln_b4 = ln_b.reshape(ln_b.shape[0], 2, 1, d)
    vec = lambda: pl.BlockSpec((None, None, 1, d), lambda i: (l, 1, 0, 0))
    return pl.pallas_call(
        functools.partial(_combine_kernel, tc=tc, alpha=alpha),
        grid=(n_tiles,),
        in_specs=[
            _first_tile_spec(tc * TOP_K), _next_tile_spec(n_tiles, tc * TOP_K),
            pl.BlockSpec((tc, d), lambda i: (i, 0)),
            pl.BlockSpec((tc, LANES), lambda i: (i, 0)),
            _mod_spec(mods, l, 1, GATE, lambda i: i // per_b),
            vec(), vec(),
            pl.BlockSpec(memory_space=pl.ANY),
        ],
        out_specs=pl.BlockSpec((tc, d), lambda i: (i, 0)),
        out_shape=jax.ShapeDtypeStruct((n, d), F32),
        scratch_shapes=[pltpu.VMEM((2, TOP_K, tc, d), F32), pltpu.SemaphoreType.DMA((2,))],
        compiler_params=_params(("arbitrary",)),
        name="combine",
    )(dest3, dest3, x2, probs, mods, ln_g4, ln_b4, ys)


def _routing_tables(top_i, cap, ch, max_groups):
    n = top_i.shape[0]
    e_flat = top_i.reshape(-1)
    onehot = (e_flat[:, None] == jnp.arange(N_EXPERTS, dtype=jnp.int32)[None, :]).astype(jnp.int32)
    csum = jnp.cumsum(onehot, axis=0)
    counts = csum[-1]
    rank = jnp.take_along_axis(csum, e_flat[:, None], axis=1)[:, 0] - 1
    groups_per = (counts + cap - 1) // cap
    group_end = jnp.cumsum(groups_per)
    group_start = group_end - groups_per
    dest = (group_start * cap)[e_flat] + rank
    n_groups = group_end[-1]
    gidx = jnp.arange(max_groups, dtype=jnp.int32)
    gclamp = jnp.minimum(gidx, n_groups - 1)
    group_expert = jnp.sum((group_end[None, :] <= gclamp[:, None]).astype(jnp.int32), axis=1)
    rows = jnp.clip(counts[group_expert] - (gclamp - group_start[group_expert]) * cap, 0, cap)
    group_chunks = jnp.where(gidx < n_groups, (rows + ch - 1) // ch, 0)
    p_rows = max_groups * cap
    tok_of_row = (jnp.arange(p_rows, dtype=jnp.int32) % n).at[dest].set(
        jnp.arange(n * TOP_K, dtype=jnp.int32) // TOP_K)
    used_rows = (n_groups * cap).reshape(1).astype(jnp.int32)
    return (dest.astype(jnp.int32), tok_of_row, used_rows,
            group_expert.astype(jnp.int32), group_chunks.astype(jnp.int32), gclamp.astype(jnp.int32))


def kernel(x, c, ada_w, ada_b, ln_g, ln_b, mix_w_in, conv_w, conv_b, lru_wa, lru_ba, lru_wx, lru_bx,
           lru_lam, pool_w, pool_b, pool_scale, mix_w_out, ffn_w_gate, ffn_w_up, ffn_w_down,
           router_w, exp_w_gate, exp_w_up, exp_w_down):
    nb, seq, d = x.shape
    depth = ada_w.shape[0]
    n = nb * seq
    alpha = float((2 * depth) ** 0.25)
    mods = _ada(c, ada_w, ada_b)
    x2 = x.reshape(n, d)

    row3 = lambda v: v.reshape(depth, 1, -1)
    seq_params = [conv_w, row3(conv_b), lru_wa.astype(BF16), row3(lru_ba), lru_wx.astype(BF16),
                  row3(lru_bx), row3(lru_lam), pool_w.astype(BF16), row3(pool_b), row3(pool_scale)]

    ffn_tf, ffn_ch = 256, 256
    dense_cap = seq
    moe_cap = 2304
    for l in range(depth):
        moe = (l % 2 == 1)
        i = l // 2

        z = _mix_in(x2, mods, l, mix_w_in, seq)
        y = _mix_seq(z, l, seq_params, nb, seq)
        outs = _mix_out(y, x2, mods, l, ln_g, ln_b, mix_w_out, router_w[i] if moe else None,
                        seq, alpha)
        if not moe:
            x2, h = outs
            n_groups = n // dense_cap
            ge = jnp.full((n_groups,), i, jnp.int32)
            gc = jnp.full((n_groups,), dense_cap // ffn_ch, jnp.int32)
            gb = jnp.arange(n_groups, dtype=jnp.int32)
            x2 = _ffn(h, ffn_w_gate, ffn_w_up, ffn_w_down, ge, gc, gb,
                      cap=dense_cap, tf=ffn_tf, ch=ffn_ch, x_buffers=2,
                      ln_args=(x2, mods, l, ln_g, ln_b), seq=seq, alpha=alpha)
        else:
            x2, h, idx, probs = outs
            max_groups = (n * TOP_K) // moe_cap + N_EXPERTS
            dest, tok_of_row, used_rows, ge, gc, gb = _routing_tables(
                idx[:, :TOP_K], moe_cap, ffn_ch, max_groups)
            xs = _gather(h, tok_of_row, used_rows, max_groups * moe_cap)
            n_exp = exp_w_gate.shape[1]
            f_exp = exp_w_gate.shape[-1]
            ys = _ffn(xs,
                      exp_w_gate.reshape(-1, d, f_exp), exp_w_up.reshape(-1, d, f_exp),
                      exp_w_down.reshape(-1, f_exp, d), ge + i * n_exp, gc, gb,
                      cap=moe_cap, tf=ffn_tf, ch=ffn_ch, x_buffers=1)
            x2 = _combine(ys, dest, x2, probs, mods, l, ln_g, ln_b, seq, alpha)
    return x2.reshape(nb, seq, d)
```

```python
import functools

import jax
import jax.numpy as jnp
from jax import lax
from jax.experimental import pallas as pl
from jax.experimental.pallas import tpu as pltpu

F32 = jnp.float32
BF16 = jnp.bfloat16

LN_EPS = 1e-5
LRU_C = 8.0
CONV_WIDTH = 4
LRU_HEADS = 4
POOL_WINDOWS = (2, 4, 8, 16)
N_EXPERTS = 8
TOP_K = 2

V7X_VMEM_LIMIT_BYTES = 58 * 1024 * 1024
LANES = 128
SUBLANES = 8
POOL_HIST = 16
SHIFT, SCALE, GATE = 0, 1, 2


def _params(sem, vmem=V7X_VMEM_LIMIT_BYTES):
    return pltpu.CompilerParams(dimension_semantics=sem, vmem_limit_bytes=vmem)


def _mod_spec(mods, l, k, which, batch_of):
    d = mods.shape[-1]
    return pl.BlockSpec((None, None, None, None, 1, d),
                        lambda *g: (l, k, batch_of(*g), which, 0, 0))


def _layer_spec(arr, l):
    nd = arr.ndim - 1
    return pl.BlockSpec((None,) + arr.shape[1:], lambda *g: (l,) + (0,) * nd)


def _ada_kernel(c_ref, w_ref, b_ref, o_ref):
    c = c_ref[...]
    c_act = (c * jax.nn.sigmoid(c)).astype(BF16)
    o_ref[...] = jnp.dot(c_act, w_ref[...].astype(BF16),
                         preferred_element_type=F32) + b_ref[...]


def _ada(c, ada_w, ada_b):
    depth, _, d, d3 = ada_w.shape
    nb = c.shape[0]
    rows = -(-nb // SUBLANES) * SUBLANES
    c_pad = jnp.pad(c, ((0, rows - nb), (0, 0)))
    w = ada_w.reshape(depth * 2, d, d3)
    b = ada_b.reshape(depth * 2, 1, d3)
    tn = 1024
    out = pl.pallas_call(
        _ada_kernel,
        grid=(depth * 2, d3 // tn),
        in_specs=[
            pl.BlockSpec((rows, d), lambda l, j: (0, 0)),
            pl.BlockSpec((None, d, tn), lambda l, j: (l, 0, j)),
            pl.BlockSpec((None, 1, tn), lambda l, j: (l, 0, j)),
        ],
        out_specs=pl.BlockSpec((None, rows, tn), lambda l, j: (l, 0, j)),
        out_shape=jax.ShapeDtypeStruct((depth * 2, rows, d3), F32),
        compiler_params=_params(("arbitrary", "arbitrary")),
        name="ada",
    )(c_pad, w, b)
    return out.reshape(depth, 2, rows, 3, 1, d)


def _to_bf16_kernel(w_ref, o_ref):
    o_ref[...] = w_ref[...].astype(BF16)


def _to_bf16(w):
    shape = w.shape
    w2 = w.reshape(-1, shape[-1])
    tr = 512
    out = pl.pallas_call(
        _to_bf16_kernel,
        grid=(w2.shape[0] // tr,),
        in_specs=[pl.BlockSpec((tr, shape[-1]), lambda i: (i, 0))],
        out_specs=pl.BlockSpec((tr, shape[-1]), lambda i: (i, 0)),
        out_shape=jax.ShapeDtypeStruct(w2.shape, BF16),
        compiler_params=_params(("arbitrary",)),
        name="to_bf16",
    )(w2)
    return out.reshape(shape)


def _block_diag_dot(v_bf16, w_ref, n_blocks):
    bd = w_ref.shape[-1]
    outs = [jnp.dot(v_bf16[:, h * bd:(h + 1) * bd], w_ref[h], preferred_element_type=F32)
            for h in range(n_blocks)]
    return jnp.concatenate(outs, axis=1)


def _neg_expm1_nonpos(v, exp_v):
    poly = 1.0 / 120.0
    for coef in (1.0 / 24.0, 1.0 / 6.0, 0.5, 1.0):
        poly = poly * v + coef
    return jnp.where(v > -0.01, -v * poly, 1.0 - exp_v)


def _lru_scan(a, u, h_carry, ts, tick):
    row = lax.broadcasted_iota(jnp.int32, (ts, 1), 0) % SUBLANES
    d = 1
    while d < SUBLANES:
        keep = row >= d
        a_prev = jnp.where(keep, pltpu.roll(a, d, 0), 1.0)
        u_prev = jnp.where(keep, pltpu.roll(u, d, 0), 0.0)
        u = u + a * u_prev
        a = a * a_prev
        d *= 2
    tick()
    carry = h_carry[0:1, :]
    groups = []
    for g in range(ts // SUBLANES):
        rows = slice(g * SUBLANES, (g + 1) * SUBLANES)
        blk = a[rows] * carry + u[rows]
        groups.append(blk)
        carry = blk[SUBLANES - 1:SUBLANES]
    h_carry[...] = jnp.broadcast_to(carry, h_carry.shape)
    return jnp.concatenate(groups, axis=0)


def _seq_stage(z_ref, small, y_ref, conv_hist, pool_hist, h_carry, t, ts, tick=lambda: None):
    cw_ref, cb_ref, wa_ref, ba_ref, wx_ref, bx_ref, lam_ref, pw_ref, pb_ref, ps_ref = small
    d_lru = cw_ref.shape[-1]
    row = lax.broadcasted_iota(jnp.int32, (ts, 1), 0)

    xl = z_ref[:, 0:d_lru]
    buf = jnp.concatenate([conv_hist[...], xl], axis=0)
    cw = cw_ref[...]
    xc = cb_ref[...] + cw[CONV_WIDTH - 1:CONV_WIDTH] * xl
    for k in range(CONV_WIDTH - 1):
        back = CONV_WIDTH - 1 - k
        xc = xc + cw[k:k + 1] * pltpu.roll(buf, back, 0)[SUBLANES:]
    conv_hist[...] = xl[ts - SUBLANES:]
    tick()

    xcb = xc.astype(BF16)
    gate_a = jax.nn.sigmoid(_block_diag_dot(xcb, wa_ref, LRU_HEADS) + ba_ref[...])
    gate_x = jax.nn.sigmoid(_block_diag_dot(xcb, wx_ref, LRU_HEADS) + bx_ref[...])
    log_a = (-LRU_C) * gate_a * jax.nn.softplus(-lam_ref[...])
    a = jnp.exp(log_a)
    mult = jnp.sqrt(_neg_expm1_nonpos(2.0 * log_a, a * a))
    u = (xc * gate_x) * mult
    tick()
    hs = _lru_scan(a, u, h_carry, ts, tick)
    y_ref[:, :d_lru] = (hs * jax.nn.gelu(z_ref[:, d_lru:2 * d_lru])).astype(y_ref.dtype)
    tick()

    xp = z_ref[:, 2 * d_lru:]
    bufp = jnp.concatenate([pool_hist[...], xp], axis=0)
    pool_hist[...] = xp[ts - POOL_HIST:]
    t_glob = t * ts + row + 1
    gd = pw_ref.shape[-1]
    ps = []
    for gi, w in enumerate(POOL_WINDOWS):
        s = bufp[:, gi * gd:(gi + 1) * gd]
        sh = 1
        while sh < w:
            s = s + pltpu.roll(s, sh, 0)
            sh *= 2
        cnt = jnp.minimum(t_glob, w).astype(F32)
        ps.append(s[POOL_HIST:] / cnt - xp[:, gi * gd:(gi + 1) * gd])
    p = jnp.concatenate(ps, axis=1).astype(BF16)
    yp = (_block_diag_dot(p, pw_ref, len(POOL_WINDOWS)) + pb_ref[...]) * ps_ref[...]
    y_ref[:, d_lru:] = yp.astype(y_ref.dtype)


def _mixer_kernel(x_ref, shift_ref, scale_ref, w_ref, *rest, ts, nt):
    small = rest[:10]
    y_ref, z_even, z_odd, conv_hist, pool_hist, h_carry = rest[10:]
    t = pl.program_id(1)

    slab = 512
    n_slabs = w_ref.shape[1] // slab

    def modulated():
        return (x_ref[...] * (1.0 + scale_ref[...]) + shift_ref[...]).astype(BF16)

    def project_slab(h, z_dst, k):
        cols = slice(k * slab, (k + 1) * slab)
        z_dst[:, cols] = jnp.dot(h, w_ref[:, cols], preferred_element_type=F32)

    def project(z_dst):
        h = modulated()
        for k in range(n_slabs):
            project_slab(h, z_dst, k)

    def sequence(z_src):
        _seq_stage(z_src, small, y_ref, conv_hist, pool_hist, h_carry, t - 1, ts)

    def both(z_dst, z_src):
        h = modulated()
        todo = list(range(n_slabs))

        def tick():
            if todo:
                project_slab(h, z_dst, todo.pop(0))

        tick()
        _seq_stage(z_src, small, y_ref, conv_hist, pool_hist, h_carry, t - 1, ts, tick)
        while todo:
            tick()

    @pl.when(t == 0)
    def _():
        conv_hist[...] = jnp.zeros_like(conv_hist)
        pool_hist[...] = jnp.zeros_like(pool_hist)
        h_carry[...] = jnp.zeros_like(h_carry)
        project(z_even)

    inner = jnp.logical_and(t > 0, t < nt)

    @pl.when(jnp.logical_and(inner, t % 2 == 1))
    def _():
        both(z_odd, z_even)

    @pl.when(jnp.logical_and(inner, t % 2 == 0))
    def _():
        both(z_even, z_odd)

    @pl.when(t == nt)
    def _():
        sequence(z_odd if (nt - 1) % 2 == 1 else z_even)


def _mixer(x2, mods, l, w_in_bf16, small, nb, seq):
    n, d = x2.shape
    d_in = w_in_bf16.shape[-1]
    d_lru = small[0].shape[-1]
    assert d_in == 3 * d_lru
    ts = 256
    nt = seq // ts
    batch_of = lambda b, t: b
    return pl.pallas_call(
        functools.partial(_mixer_kernel, ts=ts, nt=nt),
        grid=(nb, nt + 1),
        in_specs=[
            pl.BlockSpec((ts, d), lambda b, t: (b * nt + jnp.minimum(t, nt - 1), 0)),
            _mod_spec(mods, l, 0, SHIFT, batch_of),
            _mod_spec(mods, l, 0, SCALE, batch_of),
            pl.BlockSpec((None, d, d_in), lambda b, t: (l, 0, 0), pipeline_mode=pl.Buffered(1)),
        ] + [_layer_spec(a, l) for a in small],
        out_specs=pl.BlockSpec((ts, 2 * d_lru), lambda b, t: (b * nt + jnp.maximum(t - 1, 0), 0)),
        out_shape=jax.ShapeDtypeStruct((n, 2 * d_lru), BF16),
        scratch_shapes=[pltpu.VMEM((ts, d_in), F32), pltpu.VMEM((ts, d_in), F32),
                        pltpu.VMEM((SUBLANES, d_lru), F32),
                        pltpu.VMEM((POOL_HIST, d_lru), F32),
                        pltpu.VMEM((SUBLANES, d_lru), F32)],
        compiler_params=_params(("arbitrary", "arbitrary")),
        name="mixer",
    )(x2, mods, mods, w_in_bf16, *small)


def _resid_ln(x, o, gate, g, b, alpha):
    v = alpha * x + (1.0 + gate) * o
    mu = jnp.mean(v, axis=-1, keepdims=True)
    dv = v - mu
    var = jnp.mean(dv * dv, axis=-1, keepdims=True)
    return dv * lax.rsqrt(var + LN_EPS) * g + b


def _route(h, wr_ref):
    tm = h.shape[0]
    h_hi = h.astype(BF16)
    h_lo = (h - h_hi.astype(F32)).astype(BF16)
    r = jnp.dot(jnp.concatenate([h_hi, h_lo], axis=0), wr_ref[...], preferred_element_type=F32)
    r = r[:tm] + r[tm:]
    logits = r + pltpu.roll(r, LANES - N_EXPERTS, 1)
    lane = lax.broadcasted_iota(jnp.int32, logits.shape, 1).astype(F32)
    neg = jnp.float32(-jnp.inf)
    lg = jnp.where(lane < N_EXPERTS, logits, neg)
    m1 = jnp.max(lg, axis=-1, keepdims=True)
    i1 = jnp.min(jnp.where(lg == m1, lane, float(LANES)), axis=-1, keepdims=True)
    lg2 = jnp.where(lane == i1, neg, lg)
    m2 = jnp.max(lg2, axis=-1, keepdims=True)
    i2 = jnp.min(jnp.where(lg2 == m2, lane, float(LANES)), axis=-1, keepdims=True)
    e = jnp.exp(m2 - m1)
    p1 = 1.0 / (1.0 + e)
    p2 = e / (1.0 + e)
    idx = jnp.where(lane == 0, i1, i2).astype(jnp.int32)
    return idx, jnp.where(lane == 0, p1, p2)


def _router_weights(router_w):
    hi = router_w.astype(BF16)
    lo = (router_w - hi.astype(F32)).astype(BF16)
    wr = jnp.concatenate([hi, lo], axis=1)
    return jnp.pad(wr, ((0, 0), (0, LANES - wr.shape[1])))


def _mix_out_kernel(y_ref, x_ref, gate_ref, g_ref, b_ref, shift_ref, scale_ref, w_ref, *rest,
                    alpha, route):
    if route:
        wr_ref, xo_ref, h_ref, idx_ref, p_ref, w_scr = rest
    else:
        xo_ref, h_ref, w_scr = rest

    @pl.when(pl.program_id(0) == 0)
    def _():
        w_scr[...] = w_ref[...].astype(BF16)

    o = jnp.dot(y_ref[...], w_scr[...], preferred_element_type=F32)
    xn = _resid_ln(x_ref[...], o, gate_ref[...], g_ref[...], b_ref[...], alpha)
    xo_ref[...] = xn
    h = xn * (1.0 + scale_ref[...]) + shift_ref[...]
    h_ref[...] = h.astype(h_ref.dtype)
    if route:
        idx, p = _route(h, wr_ref)
        idx_ref[...] = idx
        p_ref[...] = p


def _mix_out(y, x2, mods, l, ln_g, ln_b, w_out_all, router_w, seq, alpha):
    n, d = x2.shape
    route = router_w is not None
    tm = 256
    per_b = seq // tm
    batch_of = lambda i: i // per_b
    tok = lambda: pl.BlockSpec((tm, d), lambda i: (i, 0))
    vec = lambda k: pl.BlockSpec((None, None, 1, d), lambda i: (l, k, 0, 0))
    ln_g4 = ln_g.reshape(ln_g.shape[0], 2, 1, d)
    ln_b4 = ln_b.reshape(ln_b.shape[0], 2, 1, d)
    in_specs = [tok(), tok(), _mod_spec(mods, l, 0, GATE, batch_of), vec(0), vec(0),
                _mod_spec(mods, l, 1, SHIFT, batch_of), _mod_spec(mods, l, 1, SCALE, batch_of),
                pl.BlockSpec((None, d, d), lambda i: (l, 0, 0), pipeline_mode=pl.Buffered(1))]
    args = [y, x2, mods, ln_g4, ln_b4, mods, mods, w_out_all]
    out_specs = [tok(), tok()]
    out_shape = [jax.ShapeDtypeStruct((n, d), F32),
                 jax.ShapeDtypeStruct((n, d), F32 if route else BF16)]
    if route:
        in_specs.append(pl.BlockSpec((d, LANES), lambda i: (0, 0)))
        args.append(_router_weights(router_w))
        out_specs += [pl.BlockSpec((tm, LANES), lambda i: (i, 0))] * 2
        out_shape += [jax.ShapeDtypeStruct((n, LANES), jnp.int32),
                      jax.ShapeDtypeStruct((n, LANES), F32)]
    return pl.pallas_call(
        functools.partial(_mix_out_kernel, alpha=alpha, route=route),
        grid=(n // tm,),
        in_specs=in_specs,
        out_specs=out_specs,
        out_shape=out_shape,
        scratch_shapes=[pltpu.VMEM((d, d), BF16)],
        compiler_params=_params(("arbitrary",)),
        name="mix_out",
    )(*args)


def _ffn_kernel(ge_ref, gc_ref, gb_ref, x_ref, wg_ref, wu_ref, wd_ref, *rest, ch, alpha, ln,
                static_n):
    if ln:
        xres_hbm, gate_ref, g_ref, b_ref, o_ref, wg_s, wu_s, wd_s, xr_buf, xr_sem = rest
    else:
        o_ref, wg_s, wu_s, wd_s = rest
    s = pl.program_id(0)
    j = pl.program_id(1)
    nj = pl.num_programs(1)
    n = gc_ref[s]
    cap = x_ref.shape[0]

    @pl.when(j == 0)
    def _():
        o_ref[...] = jnp.zeros_like(o_ref)

    def chunk_rows(c):
        return pl.ds(c * ch if isinstance(c, int) else pl.multiple_of(c * ch, ch), ch)

    def up(c, wg, wu):
        x = x_ref[chunk_rows(c), :]
        g = jnp.dot(x, wg, preferred_element_type=F32)
        u = jnp.dot(x, wu, preferred_element_type=F32)
        return (g * jax.nn.sigmoid(g) * u).astype(BF16)

    def down(c, hmid, wd):
        o_ref[chunk_rows(c), :] += jnp.dot(hmid, wd, preferred_element_type=F32)

    def fused(c, hmid):
        down(c - 1, hmid, wd_s[...])
        return up(c, wg_s[...], wu_s[...])

    def cast_weights():
        wg = wg_ref[...].astype(BF16)
        wu = wu_ref[...].astype(BF16)
        wd = wd_ref[...].astype(BF16)
        wg_s[...] = wg
        wu_s[...] = wu
        wd_s[...] = wd
        return wg, wu

    is_static = functools.reduce(jnp.logical_or, [n == k for k in static_n], n < 0)
    for k in static_n:
        @pl.when(n == k)
        def _(k=k):
            wg, wu = cast_weights()
            hmid = up(0, wg, wu)
            for c in range(1, k):
                hmid = fused(c, hmid)
            down(k - 1, hmid, wd_s[...])

    @pl.when(jnp.logical_and(n > 0, jnp.logical_not(is_static)))
    def _():
        wg, wu = cast_weights()
        hmid = up(0, wg, wu)

        pairs = lax.div(n - 1, 2)

        def two(p, hmid):
            return fused(2 * p + 2, fused(2 * p + 1, hmid))

        hmid = lax.fori_loop(0, pairs, two, hmid)
        hmid = lax.fori_loop(2 * pairs + 1, n, fused, hmid)
        down(n - 1, hmid, wd_s[...])

    if ln:
        row0 = gb_ref[s] * cap

        def xres_copy(c, slot):
            return pltpu.make_async_copy(
                xres_hbm.at[pl.ds(row0 + c * ch, ch)], xr_buf.at[slot], xr_sem.at[slot])

        @pl.when(jnp.logical_and(j == nj - 1, n > 0))
        def _():
            xres_copy(0, 0).start()

            def fin(c, carry):
                slot = c % 2

                @pl.when(c + 1 < n)
                def _():
                    xres_copy(c + 1, 1 - slot).start()

                xres_copy(c, slot).wait()
                rows = pl.ds(pl.multiple_of(c * ch, ch), ch)
                o_ref[rows, :] = _resid_ln(xr_buf[slot], o_ref[rows, :],
                                           gate_ref[...], g_ref[...], b_ref[...], alpha)
                return carry

            lax.fori_loop(0, n, fin, 0)


def _ffn(x_rows, w_gate, w_up, w_down, group_expert, group_chunks, group_block, *, cap, tf, ch,
         x_buffers, static_n, ln_args=None, seq=None, alpha=None):
    p_rows, d = x_rows.shape
    f = w_gate.shape[-1]
    n_groups = group_expert.shape[0]
    nj = f // tf
    ln = ln_args is not None

    def jj(s, j, gc):
        return jnp.where(gc[s] > 0, j, nj - 1)

    in_specs = [
        pl.BlockSpec((cap, d), lambda s, j, ge, gc, gb: (gb[s], 0),
                     pipeline_mode=pl.Buffered(x_buffers)),
        pl.BlockSpec((None, d, tf), lambda s, j, ge, gc, gb: (ge[s], 0, jj(s, j, gc))),
        pl.BlockSpec((None, d, tf), lambda s, j, ge, gc, gb: (ge[s], 0, jj(s, j, gc))),
        pl.BlockSpec((None, tf, d), lambda s, j, ge, gc, gb: (ge[s], jj(s, j, gc), 0)),
    ]
    args = [x_rows, w_gate, w_up, w_down]
    scratch = [pltpu.VMEM((d, tf), BF16), pltpu.VMEM((d, tf), BF16), pltpu.VMEM((tf, d), BF16)]
    if ln:
        xres, mods, l, ln_g, ln_b = ln_args
        assert seq % cap == 0
        per_b = seq // cap
        ln_g4 = ln_g.reshape(ln_g.shape[0], 2, 1, d)
        ln_b4 = ln_b.reshape(ln_b.shape[0], 2, 1, d)
        vec = lambda: pl.BlockSpec((None, None, 1, d), lambda s, j, ge, gc, gb: (l, 1, 0, 0))
        in_specs += [
            pl.BlockSpec(memory_space=pl.ANY),
            _mod_spec(mods, l, 1, GATE, lambda s, j, ge, gc, gb: gb[s] // per_b),
            vec(), vec(),
        ]
        args += [xres, mods, ln_g4, ln_b4]
        scratch += [pltpu.VMEM((2, ch, d), F32), pltpu.SemaphoreType.DMA((2,))]
    grid_spec = pltpu.PrefetchScalarGridSpec(
        num_scalar_prefetch=3,
        grid=(n_groups, nj),
        in_specs=in_specs,
        out_specs=pl.BlockSpec((cap, d), lambda s, j, ge, gc, gb: (s, 0),
                               pipeline_mode=pl.Buffered(1)),
        scratch_shapes=scratch,
    )
    return pl.pallas_call(
        functools.partial(_ffn_kernel, ch=ch, alpha=alpha, ln=ln, static_n=static_n),
        grid_spec=grid_spec,
        out_shape=jax.ShapeDtypeStruct((p_rows, d), F32),
        compiler_params=_params(("arbitrary", "arbitrary")),
        name="ffn_ln" if ln else "ffn_moe",
    )(group_expert, group_chunks, group_block, *args)


def _row_copy(src_hbm, src_row, dst_buf, dst_row, sem):
    return pltpu.make_async_copy(src_hbm.at[pl.ds(src_row, 1)], dst_buf.at[pl.ds(dst_row, 1)], sem)


def _next_tile_spec(n_tiles, width):
    return pl.BlockSpec((None, 1, width), lambda i, *_: (jnp.minimum(i + 1, n_tiles - 1), 0, 0),
                        memory_space=pltpu.SMEM)


def _first_tile_spec(width):
    return pl.BlockSpec((None, 1, width), lambda i, *_: (0, 0, 0), memory_space=pltpu.SMEM)


def _gather_kernel(used_ref, tok0_ref, tokn_ref, h_hbm, o_ref, buf, sem, *, tg):
    i = pl.program_id(0)
    n = pl.num_programs(0)

    def issue(tok_ref, slot):
        def body(r, carry):
            _row_copy(h_hbm, tok_ref[0, r], buf.at[slot], r, sem.at[slot]).start()
            return carry
        lax.fori_loop(0, tg, body, 0, unroll=8)

    def drain(slot):
        def body(r, carry):
            _row_copy(h_hbm, 0, buf.at[slot], r, sem.at[slot]).wait()
            return carry
        lax.fori_loop(0, tg, body, 0, unroll=8)

    def live(tile):
        return tile * tg < used_ref[0]

    @pl.when(jnp.logical_and(i == 0, live(0)))
    def _():
        issue(tok0_ref, 0)

    @pl.when(jnp.logical_and(i + 1 < n, live(i + 1)))
    def _():
        issue(tokn_ref, (i + 1) % 2)

    @pl.when(live(i))
    def _():
        drain(i % 2)
        o_ref[...] = buf[i % 2].astype(o_ref.dtype)

    @pl.when(jnp.logical_not(live(i)))
    def _():
        o_ref[...] = jnp.zeros_like(o_ref)


def _gather(h, tok_of_row, used_rows, p_rows):
    n, d = h.shape
    tg = 256
    n_tiles = p_rows // tg
    tok3 = tok_of_row.reshape(n_tiles, 1, tg)
    grid_spec = pltpu.PrefetchScalarGridSpec(
        num_scalar_prefetch=1,
        grid=(n_tiles,),
        in_specs=[_first_tile_spec(tg), _next_tile_spec(n_tiles, tg),
                  pl.BlockSpec(memory_space=pl.ANY)],
        out_specs=pl.BlockSpec((tg, d), lambda i, used: (i, 0)),
        scratch_shapes=[pltpu.VMEM((2, tg, d), F32), pltpu.SemaphoreType.DMA((2,))],
    )
    return pl.pallas_call(
        functools.partial(_gather_kernel, tg=tg),
        grid_spec=grid_spec,
        out_shape=jax.ShapeDtypeStruct((p_rows, d), BF16),
        compiler_params=_params(("arbitrary",)),
        name="gather",
    )(used_rows, tok3, tok3, h)


def _combine_kernel(dst0_ref, dstn_ref, x_ref, p_ref, gate_ref, g_ref, b_ref, ys_hbm, o_ref,
                    buf, sem, *, tc, alpha):
    i = pl.program_id(0)
    n = pl.num_programs(0)

    def issue(dst_ref, slot):
        def body(r, carry):
            for k in range(TOP_K):
                _row_copy(ys_hbm, dst_ref[0, r * TOP_K + k], buf.at[slot, k], r,
                          sem.at[slot]).start()
            return carry
        lax.fori_loop(0, tc, body, 0, unroll=4)

    def drain(slot):
        def body(r, carry):
            for k in range(TOP_K):
                _row_copy(ys_hbm, 0, buf.at[slot, k], r, sem.at[slot]).wait()
            return carry
        lax.fori_loop(0, tc, body, 0, unroll=4)

    @pl.when(i == 0)
    def _():
        issue(dst0_ref, 0)

    @pl.when(i + 1 < n)
    def _():
        issue(dstn_ref, (i + 1) % 2)

    slot = i % 2
    drain(slot)
    p = p_ref[...]
    o = p[:, 0:1] * buf[slot, 0] + p[:, 1:2] * buf[slot, 1]
    o_ref[...] = _resid_ln(x_ref[...], o, gate_ref[...], g_ref[...], b_ref[...], alpha)


def _combine(ys, dest, x2, probs, mods, l, ln_g, ln_b, seq, alpha):
    n, d = x2.shape
    tc = 256
    n_tiles = n // tc
    per_b = seq // tc
    dest3 = dest.reshape(n_tiles, 1, tc * TOP_K)
    ln_g4 = ln_g.reshape(ln_g.shape[0], 2, 1, d)
    ln_b4 = ln_b.reshape(ln_b.shape[0], 2, 1, d)
    vec = lambda: pl.BlockSpec((None, None, 1, d), lambda i: (l, 1, 0, 0))
    return pl.pallas_call(
        functools.partial(_combine_kernel, tc=tc, alpha=alpha),
        grid=(n_tiles,),
        in_specs=[
            _first_tile_spec(tc * TOP_K), _next_tile_spec(n_tiles, tc * TOP_K),
            pl.BlockSpec((tc, d), lambda i: (i, 0)),
            pl.BlockSpec((tc, LANES), lambda i: (i, 0)),
            _mod_spec(mods, l, 1, GATE, lambda i: i // per_b),
            vec(), vec(),
            pl.BlockSpec(memory_space=pl.ANY),
        ],
        out_specs=pl.BlockSpec((tc, d), lambda i: (i, 0)),
        out_shape=jax.ShapeDtypeStruct((n, d), F32),
        scratch_shapes=[pltpu.VMEM((2, TOP_K, tc, d), F32), pltpu.SemaphoreType.DMA((2,))],
        compiler_params=_params(("arbitrary",)),
        name="combine",
    )(dest3, dest3, x2, probs, mods, ln_g4, ln_b4, ys)


def _routing_tables(top_i, cap, ch, max_groups):
    n = top_i.shape[0]
    e_flat = top_i.reshape(-1)
    onehot = (e_flat[:, None] == jnp.arange(N_EXPERTS, dtype=jnp.int32)[None, :]).astype(jnp.int32)
    csum = jnp.cumsum(onehot, axis=0)
    counts = csum[-1]
    rank = jnp.take_along_axis(csum, e_flat[:, None], axis=1)[:, 0] - 1
    groups_per = (counts + cap - 1) // cap
    group_end = jnp.cumsum(groups_per)
    group_start = group_end - groups_per
    dest = (group_start * cap)[e_flat] + rank
    n_groups = group_end[-1]
    gidx = jnp.arange(max_groups, dtype=jnp.int32)
    gclamp = jnp.minimum(gidx, n_groups - 1)
    group_expert = jnp.sum((group_end[None, :] <= gclamp[:, None]).astype(jnp.int32), axis=1)
    rows = jnp.clip(counts[group_expert] - (gclamp - group_start[group_expert]) * cap, 0, cap)
    group_chunks = jnp.where(gidx < n_groups, (rows + ch - 1) // ch, 0)
    p_rows = max_groups * cap
    tok_of_row = (jnp.arange(p_rows, dtype=jnp.int32) % n).at[dest].set(
        jnp.arange(n * TOP_K, dtype=jnp.int32) // TOP_K)
    used_rows = (n_groups * cap).reshape(1).astype(jnp.int32)
    return (dest.astype(jnp.int32), tok_of_row, used_rows,
            group_expert.astype(jnp.int32), group_chunks.astype(jnp.int32), gclamp.astype(jnp.int32))


def kernel(x, c, ada_w, ada_b, ln_g, ln_b, mix_w_in, conv_w, conv_b, lru_wa, lru_ba, lru_wx, lru_bx,
           lru_lam, pool_w, pool_b, pool_scale, mix_w_out, ffn_w_gate, ffn_w_up, ffn_w_down,
           router_w, exp_w_gate, exp_w_up, exp_w_down):
    nb, seq, d = x.shape
    depth = ada_w.shape[0]
    n = nb * seq
    alpha = float((2 * depth) ** 0.25)
    mods = _ada(c, ada_w, ada_b)
    x2 = x.reshape(n, d)

    row3 = lambda v: v.reshape(depth, 1, -1)
    seq_params = [conv_w, row3(conv_b), lru_wa.astype(BF16), row3(lru_ba), lru_wx.astype(BF16),
                  row3(lru_bx), row3(lru_lam), pool_w.astype(BF16), row3(pool_b), row3(pool_scale)]

    w_in_bf16 = _to_bf16(mix_w_in)

    ffn_tf, ffn_ch = 256, 256
    dense_cap = seq
    moe_cap = 2304
    for l in range(depth):
        moe = (l % 2 == 1)
        i = l // 2

        y = _mixer(x2, mods, l, w_in_bf16, seq_params, nb, seq)
        outs = _mix_out(y, x2, mods, l, ln_g, ln_b, mix_w_out, router_w[i] if moe else None,
                        seq, alpha)
        if not moe:
            x2, h = outs
            n_groups = n // dense_cap
            ge = jnp.full((n_groups,), i, jnp.int32)
            gc = jnp.full((n_groups,), dense_cap // ffn_ch, jnp.int32)
            gb = jnp.arange(n_groups, dtype=jnp.int32)
            x2 = _ffn(h, ffn_w_gate, ffn_w_up, ffn_w_down, ge, gc, gb,
                      cap=dense_cap, tf=ffn_tf, ch=ffn_ch, x_buffers=2,
                      static_n=(dense_cap // ffn_ch,),
                      ln_args=(x2, mods, l, ln_g, ln_b), seq=seq, alpha=alpha)
        else:
            x2, h, idx, probs = outs
            max_groups = (n * TOP_K) // moe_cap + N_EXPERTS
            dest, tok_of_row, used_rows, ge, gc, gb = _routing_tables(
                idx[:, :TOP_K], moe_cap, ffn_ch, max_groups)
            xs = _gather(h, tok_of_row, used_rows, max_groups * moe_cap)
            n_exp = exp_w_gate.shape[1]
            f_exp = exp_w_gate.shape[-1]
            ys = _ffn(xs,
                      exp_w_gate.reshape(-1, d, f_exp), exp_w_up.reshape(-1, d, f_exp),
                      exp_w_down.reshape(-1, f_exp, d), ge + i * n_exp, gc, gb,
                      cap=moe_cap, tf=ffn_tf, ch=ffn_ch, x_buffers=1,
                      static_n=(moe_cap // ffn_ch - 1, moe_cap // ffn_ch))
            x2 = _combine(ys, dest, x2, probs, mods, l, ln_g, ln_b, seq, alpha)
    return x2.reshape(nb, seq, d)
```

```python
import functools

import jax
import jax.numpy as jnp
from jax import lax
from jax.experimental import pallas as pl
from jax.experimental.pallas import tpu as pltpu

F32 = jnp.float32
BF16 = jnp.bfloat16

LN_EPS = 1e-5
LRU_C = 8.0
CONV_WIDTH = 4
LRU_HEADS = 4
POOL_WINDOWS = (2, 4, 8, 16)
N_EXPERTS = 8
TOP_K = 2

V7X_VMEM_LIMIT_BYTES = 58 * 1024 * 1024
LANES = 128
SUBLANES = 8
POOL_HIST = 16
SHIFT, SCALE, GATE = 0, 1, 2


def _params(sem, vmem=V7X_VMEM_LIMIT_BYTES):
    return pltpu.CompilerParams(dimension_semantics=sem, vmem_limit_bytes=vmem)


def _mod_spec(mods, l, k, which, batch_of):
    d = mods.shape[-1]
    return pl.BlockSpec((None, None, None, None, 1, d),
                        lambda *g: (l, k, batch_of(*g), which, 0, 0))


def _layer_spec(arr, l):
    nd = arr.ndim - 1
    return pl.BlockSpec((None,) + arr.shape[1:], lambda *g: (l,) + (0,) * nd)


def _ada_kernel(c_ref, w_ref, b_ref, o_ref):
    c = c_ref[...]
    c_act = (c * jax.nn.sigmoid(c)).astype(BF16)
    o_ref[...] = jnp.dot(c_act, w_ref[...].astype(BF16),
                         preferred_element_type=F32) + b_ref[...]


def _ada(c, ada_w, ada_b):
    depth, _, d, d3 = ada_w.shape
    nb = c.shape[0]
    rows = -(-nb // SUBLANES) * SUBLANES
    c_pad = jnp.pad(c, ((0, rows - nb), (0, 0)))
    w = ada_w.reshape(depth * 2, d, d3)
    b = ada_b.reshape(depth * 2, 1, d3)
    tn = 1024
    out = pl.pallas_call(
        _ada_kernel,
        grid=(depth * 2, d3 // tn),
        in_specs=[
            pl.BlockSpec((rows, d), lambda l, j: (0, 0)),
            pl.BlockSpec((None, d, tn), lambda l, j: (l, 0, j)),
            pl.BlockSpec((None, 1, tn), lambda l, j: (l, 0, j)),
        ],
        out_specs=pl.BlockSpec((None, rows, tn), lambda l, j: (l, 0, j)),
        out_shape=jax.ShapeDtypeStruct((depth * 2, rows, d3), F32),
        compiler_params=_params(("arbitrary", "arbitrary")),
        name="ada",
    )(c_pad, w, b)
    return out.reshape(depth, 2, rows, 3, 1, d)


def _to_bf16_kernel(w_ref, o_ref):
    o_ref[...] = w_ref[...].astype(BF16)


def _to_bf16(w):
    shape = w.shape
    w2 = w.reshape(-1, shape[-1])
    tr = 512
    out = pl.pallas_call(
        _to_bf16_kernel,
        grid=(w2.shape[0] // tr,),
        in_specs=[pl.BlockSpec((tr, shape[-1]), lambda i: (i, 0))],
        out_specs=pl.BlockSpec((tr, shape[-1]), lambda i: (i, 0)),
        out_shape=jax.ShapeDtypeStruct(w2.shape, BF16),
        compiler_params=_params(("arbitrary",)),
        name="to_bf16",
    )(w2)
    return out.reshape(shape)


def _neg_expm1_nonpos(v, exp_v):
    poly = 1.0 / 120.0
    for coef in (1.0 / 24.0, 1.0 / 6.0, 0.5, 1.0):
        poly = poly * v + coef
    return jnp.where(v > -0.01, -v * poly, 1.0 - exp_v)


def _lru_scan(a, u, h_carry, cols, ts):
    row = lax.broadcasted_iota(jnp.int32, (ts, 1), 0) % SUBLANES
    d = 1
    while d < SUBLANES:
        keep = row >= d
        a_prev = jnp.where(keep, pltpu.roll(a, d, 0), 1.0)
        u_prev = jnp.where(keep, pltpu.roll(u, d, 0), 0.0)
        u = u + a * u_prev
        a = a * a_prev
        d *= 2
    carry = h_carry[0:1, cols]
    groups = []
    for g in range(ts // SUBLANES):
        rows = slice(g * SUBLANES, (g + 1) * SUBLANES)
        blk = a[rows] * carry + u[rows]
        groups.append(blk)
        carry = blk[SUBLANES - 1:SUBLANES]
    h_carry[:, cols] = jnp.broadcast_to(carry, (h_carry.shape[0], carry.shape[1]))
    return jnp.concatenate(groups, axis=0)


def _lru_head(z_ref, small, y_ref, conv_hist, h_carry, hd, ts):
    cw_ref, cb_ref, wa_ref, ba_ref, wx_ref, bx_ref, lam_ref = small[:7]
    d_lru = cw_ref.shape[-1]
    hdim = wa_ref.shape[-1]
    cols = slice(hd * hdim, (hd + 1) * hdim)

    xl = z_ref[:, cols]
    buf = jnp.concatenate([conv_hist[:, cols], xl], axis=0)
    cw = cw_ref[:, cols]
    xc = cb_ref[:, cols] + cw[CONV_WIDTH - 1:CONV_WIDTH] * xl
    for k in range(CONV_WIDTH - 1):
        back = CONV_WIDTH - 1 - k
        xc = xc + cw[k:k + 1] * pltpu.roll(buf, back, 0)[SUBLANES:]
    conv_hist[:, cols] = xl[ts - SUBLANES:]

    xcb = xc.astype(BF16)
    gate_a = jax.nn.sigmoid(
        jnp.dot(xcb, wa_ref[hd], preferred_element_type=F32) + ba_ref[:, cols])
    gate_x = jax.nn.sigmoid(
        jnp.dot(xcb, wx_ref[hd], preferred_element_type=F32) + bx_ref[:, cols])
    log_a = (-LRU_C) * gate_a * jax.nn.softplus(-lam_ref[:, cols])
    a = jnp.exp(log_a)
    mult = jnp.sqrt(_neg_expm1_nonpos(2.0 * log_a, a * a))
    u = (xc * gate_x) * mult
    hs = _lru_scan(a, u, h_carry, cols, ts)
    gl = z_ref[:, d_lru + hd * hdim:d_lru + (hd + 1) * hdim]
    y_ref[:, cols] = (hs * jax.nn.gelu(gl)).astype(y_ref.dtype)


def _pool_group(z_ref, small, y_ref, pool_hist, gi, t, ts):
    d_lru = small[0].shape[-1]
    pw_ref, pb_ref, ps_ref = small[7:]
    gd = pw_ref.shape[-1]
    w = POOL_WINDOWS[gi]
    cols = slice(gi * gd, (gi + 1) * gd)
    row = lax.broadcasted_iota(jnp.int32, (ts, 1), 0)
    xp = z_ref[:, 2 * d_lru + gi * gd:2 * d_lru + (gi + 1) * gd]
    s = jnp.concatenate([pool_hist[:, cols], xp], axis=0)
    pool_hist[:, cols] = xp[ts - POOL_HIST:]
    sh = 1
    while sh < w:
        s = s + pltpu.roll(s, sh, 0)
        sh *= 2
    cnt = jnp.minimum(t * ts + row + 1, w).astype(F32)
    p = (s[POOL_HIST:] / cnt - xp).astype(BF16)
    yp = (jnp.dot(p, pw_ref[gi], preferred_element_type=F32) + pb_ref[:, cols]) * ps_ref[:, cols]
    y_ref[:, d_lru + gi * gd:d_lru + (gi + 1) * gd] = yp.astype(y_ref.dtype)


def _mixer_kernel(x_ref, shift_ref, scale_ref, w_ref, *rest, ts, nt):
    small = rest[:10]
    y_ref, z_even, z_odd, h_mod, conv_hist, pool_hist, h_carry = rest[10:]
    t = pl.program_id(1)
    n_parts = LRU_HEADS
    assert len(POOL_WINDOWS) == n_parts
    slab = w_ref.shape[1] // n_parts

    def modulate():
        h_mod[...] = (x_ref[...] * (1.0 + scale_ref[...]) + shift_ref[...]).astype(BF16)

    def part(i, z_dst, z_src):
        if z_dst is not None:
            cols = slice(i * slab, (i + 1) * slab)
            z_dst[:, cols] = jnp.dot(h_mod[...], w_ref[:, cols], preferred_element_type=F32)
        if z_src is not None:
            _lru_head(z_src, small, y_ref, conv_hist, h_carry, i, ts)
            _pool_group(z_src, small, y_ref, pool_hist, i, t - 1, ts)

    def parts(z_dst, z_src):
        for i in range(n_parts):
            part(i, z_dst, z_src)

    @pl.when(t == 0)
    def _():
        conv_hist[...] = jnp.zeros_like(conv_hist)
        pool_hist[...] = jnp.zeros_like(pool_hist)
        h_carry[...] = jnp.zeros_like(h_carry)
        modulate()
        parts(z_even, None)

    inner = jnp.logical_and(t > 0, t < nt)

    @pl.when(jnp.logical_and(inner, t % 2 == 1))
    def _():
        modulate()
        parts(z_odd, z_even)

    @pl.when(jnp.logical_and(inner, t % 2 == 0))
    def _():
        modulate()
        parts(z_even, z_odd)

    @pl.when(t == nt)
    def _():
        parts(None, z_odd if (nt - 1) % 2 == 1 else z_even)


def _mixer(x2, mods, l, w_in_bf16, small, nb, seq):
    n, d = x2.shape
    d_in = w_in_bf16.shape[-1]
    d_lru = small[0].shape[-1]
    assert d_in == 3 * d_lru
    ts = 256
    nt = seq // ts
    batch_of = lambda b, t: b
    return pl.pallas_call(
        functools.partial(_mixer_kernel, ts=ts, nt=nt),
        grid=(nb, nt + 1),
        in_specs=[
            pl.BlockSpec((ts, d), lambda b, t: (b * nt + jnp.minimum(t, nt - 1), 0)),
            _mod_spec(mods, l, 0, SHIFT, batch_of),
            _mod_spec(mods, l, 0, SCALE, batch_of),
            pl.BlockSpec((None, d, d_in), lambda b, t: (l, 0, 0), pipeline_mode=pl.Buffered(1)),
        ] + [_layer_spec(a, l) for a in small],
        out_specs=pl.BlockSpec((ts, 2 * d_lru), lambda b, t: (b * nt + jnp.maximum(t - 1, 0), 0)),
        out_shape=jax.ShapeDtypeStruct((n, 2 * d_lru), BF16),
        scratch_shapes=[pltpu.VMEM((ts, d_in), F32), pltpu.VMEM((ts, d_in), F32),
                        pltpu.VMEM((ts, d), BF16),
                        pltpu.VMEM((SUBLANES, d_lru), F32),
                        pltpu.VMEM((POOL_HIST, d_lru), F32),
                        pltpu.VMEM((SUBLANES, d_lru), F32)],
        compiler_params=_params(("arbitrary", "arbitrary")),
        name="mixer",
    )(x2, mods, mods, w_in_bf16, *small)


def _resid_ln(x, o, gate, g, b, alpha):
    v = alpha * x + (1.0 + gate) * o
    mu = jnp.mean(v, axis=-1, keepdims=True)
    dv = v - mu
    var = jnp.mean(dv * dv, axis=-1, keepdims=True)
    return dv * lax.rsqrt(var + LN_EPS) * g + b


def _route(h, wr_ref):
    tm = h.shape[0]
    h_hi = h.astype(BF16)
    h_lo = (h - h_hi.astype(F32)).astype(BF16)
    r = jnp.dot(jnp.concatenate([h_hi, h_lo], axis=0), wr_ref[...], preferred_element_type=F32)
    r = r[:tm] + r[tm:]
    logits = r + pltpu.roll(r, LANES - N_EXPERTS, 1)
    lane = lax.broadcasted_iota(jnp.int32, logits.shape, 1).astype(F32)
    neg = jnp.float32(-jnp.inf)
    lg = jnp.where(lane < N_EXPERTS, logits, neg)
    m1 = jnp.max(lg, axis=-1, keepdims=True)
    i1 = jnp.min(jnp.where(lg == m1, lane, float(LANES)), axis=-1, keepdims=True)
    lg2 = jnp.where(lane == i1, neg, lg)
    m2 = jnp.max(lg2, axis=-1, keepdims=True)
    i2 = jnp.min(jnp.where(lg2 == m2, lane, float(LANES)), axis=-1, keepdims=True)
    e = jnp.exp(m2 - m1)
    p1 = 1.0 / (1.0 + e)
    p2 = e / (1.0 + e)
    idx = jnp.where(lane == 0, i1, i2).astype(jnp.int32)
    return idx, jnp.where(lane == 0, p1, p2)


def _router_weights(router_w):
    hi = router_w.astype(BF16)
    lo = (router_w - hi.astype(F32)).astype(BF16)
    wr = jnp.concatenate([hi, lo], axis=1)
    return jnp.pad(wr, ((0, 0), (0, LANES - wr.shape[1])))


def _mix_out_kernel(y_ref, x_ref, gate_ref, g_ref, b_ref, shift_ref, scale_ref, w_ref, *rest,
                    alpha, route):
    if route:
        wr_ref, xo_ref, h_ref, idx_ref, p_ref, w_scr = rest
    else:
        xo_ref, h_ref, w_scr = rest

    @pl.when(pl.program_id(0) == 0)
    def _():
        w_scr[...] = w_ref[...].astype(BF16)

    o = jnp.dot(y_ref[...], w_scr[...], preferred_element_type=F32)
    xn = _resid_ln(x_ref[...], o, gate_ref[...], g_ref[...], b_ref[...], alpha)
    xo_ref[...] = xn
    h = xn * (1.0 + scale_ref[...]) + shift_ref[...]
    h_ref[...] = h.astype(h_ref.dtype)
    if route:
        idx, p = _route(h, wr_ref)
        idx_ref[...] = idx
        p_ref[...] = p


def _mix_out(y, x2, mods, l, ln_g, ln_b, w_out_all, router_w, seq, alpha):
    n, d = x2.shape
    route = router_w is not None
    tm = 256
    per_b = seq // tm
    batch_of = lambda i: i // per_b
    tok = lambda: pl.BlockSpec((tm, d), lambda i: (i, 0))
    vec = lambda k: pl.BlockSpec((None, None, 1, d), lambda i: (l, k, 0, 0))
    ln_g4 = ln_g.reshape(ln_g.shape[0], 2, 1, d)
    ln_b4 = ln_b.reshape(ln_b.shape[0], 2, 1, d)
    in_specs = [tok(), tok(), _mod_spec(mods, l, 0, GATE, batch_of), vec(0), vec(0),
                _mod_spec(mods, l, 1, SHIFT, batch_of), _mod_spec(mods, l, 1, SCALE, batch_of),
                pl.BlockSpec((None, d, d), lambda i: (l, 0, 0), pipeline_mode=pl.Buffered(1))]
    args = [y, x2, mods, ln_g4, ln_b4, mods, mods, w_out_all]
    out_specs = [tok(), tok()]
    out_shape = [jax.ShapeDtypeStruct((n, d), F32),
                 jax.ShapeDtypeStruct((n, d), F32 if route else BF16)]
    if route:
        in_specs.append(pl.BlockSpec((d, LANES), lambda i: (0, 0)))
        args.append(_router_weights(router_w))
        out_specs += [pl.BlockSpec((tm, LANES), lambda i: (i, 0))] * 2
        out_shape += [jax.ShapeDtypeStruct((n, LANES), jnp.int32),
                      jax.ShapeDtypeStruct((n, LANES), F32)]
    return pl.pallas_call(
        functools.partial(_mix_out_kernel, alpha=alpha, route=route),
        grid=(n // tm,),
        in_specs=in_specs,
        out_specs=out_specs,
        out_shape=out_shape,
        scratch_shapes=[pltpu.VMEM((d, d), BF16)],
        compiler_params=_params(("arbitrary",)),
        name="mix_out",
    )(*args)


def _ffn_kernel(ge_ref, gc_ref, gb_ref, gv_ref, x_ref, wg0_ref, wu0_ref, wd0_ref, wgn_ref, wun_ref,
                wdn_ref, *rest, ch, alpha, ln, static_n):
    if ln:
        xres_hbm, gate_ref, g_ref, b_ref, o_ref = rest[:5]
        bufs, (xr_buf, xr_sem) = rest[5:11], rest[11:]
    else:
        o_ref, bufs = rest[0], rest[1:7]
    w_bf16 = (bufs[:3], bufs[3:])
    s = pl.program_id(0)
    j = pl.program_id(1)
    nj = pl.num_programs(1)
    n = gc_ref[s]
    cap = x_ref.shape[0]

    @pl.when(j == 0)
    def _():
        o_ref[...] = jnp.zeros_like(o_ref)

    @pl.when(jnp.logical_and(s == 0, j == 0))
    def _():
        for dst, src in zip(w_bf16[0], (wg0_ref, wu0_ref, wd0_ref)):
            dst[...] = src[...].astype(BF16)

    def chunk_rows(c):
        return pl.ds(c * ch if isinstance(c, int) else pl.multiple_of(c * ch, ch), ch)

    def step(parity, static_k):
        wg_s, wu_s, wd_s = w_bf16[parity]

        def up(c):
            x = x_ref[chunk_rows(c), :]
            g = jnp.dot(x, wg_s[...], preferred_element_type=F32)
            u = jnp.dot(x, wu_s[...], preferred_element_type=F32)
            return (g * jax.nn.sigmoid(g) * u).astype(BF16)

        def down(c, hmid):
            o_ref[chunk_rows(c), :] += jnp.dot(hmid, wd_s[...], preferred_element_type=F32)

        def fused(c, hmid):
            down(c - 1, hmid)
            return up(c)

        def convert_next(piece, n_pieces):
            for dst, src in zip(w_bf16[1 - parity], (wgn_ref, wun_ref, wdn_ref)):
                rows = src.shape[0] // n_pieces
                sl = slice(piece * rows, (piece + 1) * rows)
                dst[sl, :] = src[sl, :].astype(BF16)

        if static_k is not None and static_k >= 3:
            n_pieces = min(4, static_k - 2)
            pending = [up(0), up(1)]
            for c in range(2, static_k):
                down(c - 2, pending.pop(0))
                pending.append(up(c))
                if c - 2 < n_pieces:
                    convert_next(c - 2, n_pieces)
            down(static_k - 2, pending.pop(0))
            down(static_k - 1, pending.pop(0))
            return

        hmid = up(0)
        if static_k is not None:
            convert_next(0, 1)
            for c in range(1, static_k):
                hmid = fused(c, hmid)
            down(static_k - 1, hmid)
        else:
            convert_next(0, 1)
            pairs = lax.div(n - 1, 2)

            def two(p, hmid):
                return fused(2 * p + 2, fused(2 * p + 1, hmid))

            hmid = lax.fori_loop(0, pairs, two, hmid)
            hmid = lax.fori_loop(2 * pairs + 1, n, fused, hmid)
            down(n - 1, hmid)

    is_static = functools.reduce(jnp.logical_or, [n == k for k in static_n], n < 0)
    for parity in (0, 1):
        mine = j % 2 == parity
        for k in static_n:
            pl.when(jnp.logical_and(mine, n == k))(functools.partial(step, parity, k))
        pl.when(jnp.logical_and(mine, jnp.logical_and(n > 0, jnp.logical_not(is_static))))(
            functools.partial(step, parity, None))

    if ln:
        row0 = gb_ref[s] * cap

        def xres_copy(c, slot):
            return pltpu.make_async_copy(
                xres_hbm.at[pl.ds(row0 + c * ch, ch)], xr_buf.at[slot], xr_sem.at[slot])

        @pl.when(jnp.logical_and(j == nj - 1, n > 0))
        def _():
            xres_copy(0, 0).start()

            def fin(c, carry):
                slot = c % 2

                @pl.when(c + 1 < n)
                def _():
                    xres_copy(c + 1, 1 - slot).start()

                xres_copy(c, slot).wait()
                rows = pl.ds(pl.multiple_of(c * ch, ch), ch)
                o_ref[rows, :] = _resid_ln(xr_buf[slot], o_ref[rows, :],
                                           gate_ref[...], g_ref[...], b_ref[...], alpha)
                return carry

            lax.fori_loop(0, n, fin, 0)


def _ffn(x_rows, w_gate, w_up, w_down, group_expert, group_chunks, group_block, groups_used, *,
         cap, tf, ch, static_n, ln_args=None, seq=None, alpha=None):
    p_rows, d = x_rows.shape
    f = w_gate.shape[-1]
    n_groups = group_expert.shape[0]
    nj = f // tf
    assert nj % 2 == 0
    ln = ln_args is not None

    def following(s, j, ge, gc, gv):
        wrap = j + 1 >= nj
        s2 = jnp.where(wrap, s + 1, s)
        j2 = jnp.where(wrap, 0, j + 1)
        live = s2 < gv[0]
        last = gv[0] - 1
        return ge[jnp.where(live, s2, last)], jnp.where(live, j2, nj - 1)

    def first_up(s, j, ge, gc, gb, gv):
        return ge[0], 0, 0

    def first_down(s, j, ge, gc, gb, gv):
        return ge[0], 0, 0

    def next_up(s, j, ge, gc, gb, gv):
        e, j2 = following(s, j, ge, gc, gv)
        return e, 0, j2

    def next_down(s, j, ge, gc, gb, gv):
        e, j2 = following(s, j, ge, gc, gv)
        return e, j2, 0

    once = pl.Buffered(1)
    in_specs = [
        pl.BlockSpec((cap, d), lambda s, j, ge, gc, gb, gv: (gb[s], 0), pipeline_mode=once),
        pl.BlockSpec((None, d, tf), first_up, pipeline_mode=once),
        pl.BlockSpec((None, d, tf), first_up, pipeline_mode=once),
        pl.BlockSpec((None, tf, d), first_down, pipeline_mode=once),
        pl.BlockSpec((None, d, tf), next_up),
        pl.BlockSpec((None, d, tf), next_up),
        pl.BlockSpec((None, tf, d), next_down),
    ]
    args = [x_rows, w_gate, w_up, w_down, w_gate, w_up, w_down]
    scratch = 2 * [pltpu.VMEM((d, tf), BF16), pltpu.VMEM((d, tf), BF16), pltpu.VMEM((tf, d), BF16)]
    if ln:
        xres, mods, l, ln_g, ln_b = ln_args
        assert seq % cap == 0
        per_b = seq // cap
        ln_g4 = ln_g.reshape(ln_g.shape[0], 2, 1, d)
        ln_b4 = ln_b.reshape(ln_b.shape[0], 2, 1, d)
        vec = lambda: pl.BlockSpec((None, None, 1, d), lambda s, j, ge, gc, gb, gv: (l, 1, 0, 0))
        in_specs += [
            pl.BlockSpec(memory_space=pl.ANY),
            _mod_spec(mods, l, 1, GATE, lambda s, j, ge, gc, gb, gv: gb[s] // per_b),
            vec(), vec(),
        ]
        args += [xres, mods, ln_g4, ln_b4]
        scratch += [pltpu.VMEM((2, ch, d), F32), pltpu.SemaphoreType.DMA((2,))]
    grid_spec = pltpu.PrefetchScalarGridSpec(
        num_scalar_prefetch=4,
        grid=(n_groups, nj),
        in_specs=in_specs,
        out_specs=pl.BlockSpec((cap, d), lambda s, j, ge, gc, gb, gv: (s, 0), pipeline_mode=once),
        scratch_shapes=scratch,
    )
    return pl.pallas_call(
        functools.partial(_ffn_kernel, ch=ch, alpha=alpha, ln=ln, static_n=static_n),
        grid_spec=grid_spec,
        out_shape=jax.ShapeDtypeStruct((p_rows, d), F32),
        compiler_params=_params(("arbitrary", "arbitrary")),
        name="ffn_ln" if ln else "ffn_moe",
    )(group_expert, group_chunks, group_block, groups_used, *args)


def _row_copy(src_hbm, src_row, dst_buf, dst_row, sem):
    return pltpu.make_async_copy(src_hbm.at[pl.ds(src_row, 1)], dst_buf.at[pl.ds(dst_row, 1)], sem)


def _next_tile_spec(n_tiles, width):
    return pl.BlockSpec((None, 1, width), lambda i, *_: (jnp.minimum(i + 1, n_tiles - 1), 0, 0),
                        memory_space=pltpu.SMEM)


def _first_tile_spec(width):
    return pl.BlockSpec((None, 1, width), lambda i, *_: (0, 0, 0), memory_space=pltpu.SMEM)


def _gather_kernel(used_ref, tok0_ref, tokn_ref, h_hbm, o_ref, buf, sem, *, tg):
    i = pl.program_id(0)
    n = pl.num_programs(0)

    def issue(tok_ref, slot):
        def body(r, carry):
            _row_copy(h_hbm, tok_ref[0, r], buf.at[slot], r, sem.at[slot]).start()
            return carry
        lax.fori_loop(0, tg, body, 0, unroll=8)

    def drain(slot):
        def body(r, carry):
            _row_copy(h_hbm, 0, buf.at[slot], r, sem.at[slot]).wait()
            return carry
        lax.fori_loop(0, tg, body, 0, unroll=8)

    def live(tile):
        return tile * tg < used_ref[0]

    @pl.when(jnp.logical_and(i == 0, live(0)))
    def _():
        issue(tok0_ref, 0)

    @pl.when(jnp.logical_and(i + 1 < n, live(i + 1)))
    def _():
        issue(tokn_ref, (i + 1) % 2)

    @pl.when(live(i))
    def _():
        drain(i % 2)
        o_ref[...] = buf[i % 2].astype(o_ref.dtype)

    @pl.when(jnp.logical_not(live(i)))
    def _():
        o_ref[...] = jnp.zeros_like(o_ref)


def _gather(h, tok_of_row, used_rows, p_rows):
    n, d = h.shape
    tg = 256
    n_tiles = p_rows // tg
    tok3 = tok_of_row.reshape(n_tiles, 1, tg)
    grid_spec = pltpu.PrefetchScalarGridSpec(
        num_scalar_prefetch=1,
        grid=(n_tiles,),
        in_specs=[_first_tile_spec(tg), _next_tile_spec(n_tiles, tg),
                  pl.BlockSpec(memory_space=pl.ANY)],
        out_specs=pl.BlockSpec((tg, d), lambda i, used: (i, 0)),
        scratch_shapes=[pltpu.VMEM((2, tg, d), F32), pltpu.SemaphoreType.DMA((2,))],
    )
    return pl.pallas_call(
        functools.partial(_gather_kernel, tg=tg),
        grid_spec=grid_spec,
        out_shape=jax.ShapeDtypeStruct((p_rows, d), BF16),
        compiler_params=_params(("arbitrary",)),
        name="gather",
    )(used_rows, tok3, tok3, h)


def _combine_kernel(dst0_ref, dstn_ref, x_ref, p_ref, gate_ref, g_ref, b_ref, ys_hbm, o_ref,
                    buf, sem, *, tc, alpha):
    i = pl.program_id(0)
    n = pl.num_programs(0)

    def issue(dst_ref, slot):
        def body(r, carry):
            for k in range(TOP_K):
                _row_copy(ys_hbm, dst_ref[0, r * TOP_K + k], buf.at[slot, k], r,
                          sem.at[slot]).start()
            return carry
        lax.fori_loop(0, tc, body, 0, unroll=4)

    def drain(slot):
        def body(r, carry):
            for k in range(TOP_K):
                _row_copy(ys_hbm, 0, buf.at[slot, k], r, sem.at[slot]).wait()
            return carry
        lax.fori_loop(0, tc, body, 0, unroll=4)

    @pl.when(i == 0)
    def _():
        issue(dst0_ref, 0)

    @pl.when(i + 1 < n)
    def _():
        issue(dstn_ref, (i + 1) % 2)

    slot = i % 2
    drain(slot)
    p = p_ref[...]
    o = p[:, 0:1] * buf[slot, 0] + p[:, 1:2] * buf[slot, 1]
    o_ref[...] = _resid_ln(x_ref[...], o, gate_ref[...], g_ref[...], b_ref[...], alpha)


def _combine(ys, dest, x2, probs, mods, l, ln_g, ln_b, seq, alpha):
    n, d = x2.shape
    tc = 256
    n_tiles = n // tc
    per_b = seq // tc
    dest3 = dest.reshape(n_tiles, 1, tc * TOP_K)
    ln_g4 = ln_g.reshape(ln_g.shape[0], 2, 1, d)
    ln_b4 = ln_b.reshape(ln_b.shape[0], 2, 1, d)
    vec = lambda: pl.BlockSpec((None, None, 1, d), lambda i: (l, 1, 0, 0))
    return pl.pallas_call(
        functools.partial(_combine_kernel, tc=tc, alpha=alpha),
        grid=(n_tiles,),
        in_specs=[
            _first_tile_spec(tc * TOP_K), _next_tile_spec(n_tiles, tc * TOP_K),
            pl.BlockSpec((tc, d), lambda i: (i, 0)),
            pl.BlockSpec((tc, LANES), lambda i: (i, 0)),
            _mod_spec(mods, l, 1, GATE, lambda i: i // per_b),
            vec(), vec(),
            pl.BlockSpec(memory_space=pl.ANY),
        ],
        out_specs=pl.BlockSpec((tc, d), lambda i: (i, 0)),
        out_shape=jax.ShapeDtypeStruct((n, d), F32),
        scratch_shapes=[pltpu.VMEM((2, TOP_K, tc, d), F32), pltpu.SemaphoreType.DMA((2,))],
        compiler_params=_params(("arbitrary",)),
        name="combine",
    )(dest3, dest3, x2, probs, mods, ln_g4, ln_b4, ys)


def _routing_tables(top_i, cap, ch, max_groups):
    n = top_i.shape[0]
    e_flat = top_i.reshape(-1)
    onehot = (e_flat[:, None] == jnp.arange(N_EXPERTS, dtype=jnp.int32)[None, :]).astype(jnp.int32)
    csum = jnp.cumsum(onehot, axis=0)
    counts = csum[-1]
    rank = jnp.take_along_axis(csum, e_flat[:, None], axis=1)[:, 0] - 1
    groups_per = (counts + cap - 1) // cap
    group_end = jnp.cumsum(groups_per)
    group_start = group_end - groups_per
    dest = (group_start * cap)[e_flat] + rank
    n_groups = group_end[-1]
    gidx = jnp.arange(max_groups, dtype=jnp.int32)
    gclamp = jnp.minimum(gidx, n_groups - 1)
    group_expert = jnp.sum((group_end[None, :] <= gclamp[:, None]).astype(jnp.int32), axis=1)
    rows = jnp.clip(counts[group_expert] - (gclamp - group_start[group_expert]) * cap, 0, cap)
    group_chunks = jnp.where(gidx < n_groups, (rows + ch - 1) // ch, 0)
    p_rows = max_groups * cap
    tok_of_row = (jnp.arange(p_rows, dtype=jnp.int32) % n).at[dest].set(
        jnp.arange(n * TOP_K, dtype=jnp.int32) // TOP_K)
    used_rows = (n_groups * cap).reshape(1).astype(jnp.int32)
    return (dest.astype(jnp.int32), tok_of_row, used_rows,
            group_expert.astype(jnp.int32), group_chunks.astype(jnp.int32), gclamp.astype(jnp.int32),
            n_groups.reshape(1).astype(jnp.int32))


def kernel(x, c, ada_w, ada_b, ln_g, ln_b, mix_w_in, conv_w, conv_b, lru_wa, lru_ba, lru_wx, lru_bx,
           lru_lam, pool_w, pool_b, pool_scale, mix_w_out, ffn_w_gate, ffn_w_up, ffn_w_down,
           router_w, exp_w_gate, exp_w_up, exp_w_down):
    nb, seq, d = x.shape
    depth = ada_w.shape[0]
    n = nb * seq
    alpha = float((2 * depth) ** 0.25)
    mods = _ada(c, ada_w, ada_b)
    x2 = x.reshape(n, d)

    row3 = lambda v: v.reshape(depth, 1, -1)
    seq_params = [conv_w, row3(conv_b), lru_wa.astype(BF16), row3(lru_ba), lru_wx.astype(BF16),
                  row3(lru_bx), row3(lru_lam), pool_w.astype(BF16), row3(pool_b), row3(pool_scale)]

    w_in_bf16 = _to_bf16(mix_w_in)

    ffn_tf, ffn_ch = 256, 256
    dense_cap = seq
    moe_cap = 2304
    for l in range(depth):
        moe = (l % 2 == 1)
        i = l // 2

        y = _mixer(x2, mods, l, w_in_bf16, seq_params, nb, seq)
        outs = _mix_out(y, x2, mods, l, ln_g, ln_b, mix_w_out, router_w[i] if moe else None,
                        seq, alpha)
        if not moe:
            x2, h = outs
            n_groups = n // dense_cap
            ge = jnp.full((n_groups,), i, jnp.int32)
            gc = jnp.full((n_groups,), dense_cap // ffn_ch, jnp.int32)
            gb = jnp.arange(n_groups, dtype=jnp.int32)
            gv = jnp.full((1,), n_groups, jnp.int32)
            x2 = _ffn(h, ffn_w_gate, ffn_w_up, ffn_w_down, ge, gc, gb, gv,
                      cap=dense_cap, tf=ffn_tf, ch=ffn_ch, static_n=(dense_cap // ffn_ch,),
                      ln_args=(x2, mods, l, ln_g, ln_b), seq=seq, alpha=alpha)
        else:
            x2, h, idx, probs = outs
            max_groups = (n * TOP_K) // moe_cap + N_EXPERTS
            dest, tok_of_row, used_rows, ge, gc, gb, gv = _routing_tables(
                idx[:, :TOP_K], moe_cap, ffn_ch, max_groups)
            xs = _gather(h, tok_of_row, used_rows, max_groups * moe_cap)
            n_exp = exp_w_gate.shape[1]
            f_exp = exp_w_gate.shape[-1]
            ys = _ffn(xs,
                      exp_w_gate.reshape(-1, d, f_exp), exp_w_up.reshape(-1, d, f_exp),
                      exp_w_down.reshape(-1, f_exp, d), ge + i * n_exp, gc, gb, gv,
                      cap=moe_cap, tf=ffn_tf, ch=ffn_ch,
                      static_n=(moe_cap // ffn_ch - 1, moe_cap // ffn_ch))
            x2 = _combine(ys, dest, x2, probs, mods, l, ln_g, ln_b, seq, alpha)
    return x2.reshape(nb, seq, d)
```

```python
import functools

import jax
import jax.numpy as jnp
from jax import lax
from jax.experimental import pallas as pl
from jax.experimental.pallas import tpu as pltpu

F32 = jnp.float32
BF16 = jnp.bfloat16

LN_EPS = 1e-5
LRU_C = 8.0
CONV_WIDTH = 4
LRU_HEADS = 4
POOL_WINDOWS = (2, 4, 8, 16)
N_EXPERTS = 8
TOP_K = 2

V7X_VMEM_LIMIT_BYTES = 58 * 1024 * 1024
LANES = 128
SUBLANES = 8
POOL_HIST = 16
SHIFT, SCALE, GATE = 0, 1, 2
GATHER_TILE = 256
ROW_BURST = 8


def _params(sem, vmem=V7X_VMEM_LIMIT_BYTES):
    return pltpu.CompilerParams(dimension_semantics=sem, vmem_limit_bytes=vmem)


def _mod_spec(mods, l, k, which, batch_of):
    d = mods.shape[-1]
    return pl.BlockSpec((None, None, None, None, 1, d),
                        lambda *g: (l, k, batch_of(*g), which, 0, 0))


def _layer_spec(arr, l):
    nd = arr.ndim - 1
    return pl.BlockSpec((None,) + arr.shape[1:], lambda *g: (l,) + (0,) * nd)


def _ada_kernel(c_ref, w_ref, b_ref, o_ref):
    c = c_ref[...]
    c_act = (c * jax.nn.sigmoid(c)).astype(BF16)
    o_ref[...] = jnp.dot(c_act, w_ref[...].astype(BF16),
                         preferred_element_type=F32) + b_ref[...]


def _ada(c, ada_w, ada_b):
    depth, _, d, d3 = ada_w.shape
    nb = c.shape[0]
    rows = -(-nb // SUBLANES) * SUBLANES
    c_pad = jnp.pad(c, ((0, rows - nb), (0, 0)))
    w = ada_w.reshape(depth * 2, d, d3)
    b = ada_b.reshape(depth * 2, 1, d3)
    tn = 1024
    out = pl.pallas_call(
        _ada_kernel,
        grid=(depth * 2, d3 // tn),
        in_specs=[
            pl.BlockSpec((rows, d), lambda l, j: (0, 0)),
            pl.BlockSpec((None, d, tn), lambda l, j: (l, 0, j)),
            pl.BlockSpec((None, 1, tn), lambda l, j: (l, 0, j)),
        ],
        out_specs=pl.BlockSpec((None, rows, tn), lambda l, j: (l, 0, j)),
        out_shape=jax.ShapeDtypeStruct((depth * 2, rows, d3), F32),
        compiler_params=_params(("arbitrary", "arbitrary")),
        name="ada",
    )(c_pad, w, b)
    return out.reshape(depth, 2, rows, 3, 1, d)


def _to_bf16_kernel(w_ref, o_ref):
    o_ref[...] = w_ref[...].astype(BF16)


def _to_bf16(w):
    shape = w.shape
    w2 = w.reshape(-1, shape[-1])
    tr = 512
    out = pl.pallas_call(
        _to_bf16_kernel,
        grid=(w2.shape[0] // tr,),
        in_specs=[pl.BlockSpec((tr, shape[-1]), lambda i: (i, 0))],
        out_specs=pl.BlockSpec((tr, shape[-1]), lambda i: (i, 0)),
        out_shape=jax.ShapeDtypeStruct(w2.shape, BF16),
        compiler_params=_params(("arbitrary",)),
        name="to_bf16",
    )(w2)
    return out.reshape(shape)


def _neg_expm1_nonpos(v, exp_v):
    poly = 1.0 / 120.0
    for coef in (1.0 / 24.0, 1.0 / 6.0, 0.5, 1.0):
        poly = poly * v + coef
    return jnp.where(v > -0.01, -v * poly, 1.0 - exp_v)


def _lru_scan(a, u, h_carry, cols, ts):
    row = lax.broadcasted_iota(jnp.int32, (ts, 1), 0) % SUBLANES
    d = 1
    while d < SUBLANES:
        keep = row >= d
        a_prev = jnp.where(keep, pltpu.roll(a, d, 0), 1.0)
        u_prev = jnp.where(keep, pltpu.roll(u, d, 0), 0.0)
        u = u + a * u_prev
        a = a * a_prev
        d *= 2
    carry = h_carry[0:1, cols]
    groups = []
    for g in range(ts // SUBLANES):
        rows = slice(g * SUBLANES, (g + 1) * SUBLANES)
        blk = a[rows] * carry + u[rows]
        groups.append(blk)
        carry = blk[SUBLANES - 1:SUBLANES]
    h_carry[:, cols] = jnp.broadcast_to(carry, (h_carry.shape[0], carry.shape[1]))
    return jnp.concatenate(groups, axis=0)


def _lru_head(z_ref, small, y_ref, conv_hist, h_carry, hd, ts):
    cw_ref, cb_ref, wa_ref, ba_ref, wx_ref, bx_ref, lam_ref = small[:7]
    d_lru = cw_ref.shape[-1]
    hdim = wa_ref.shape[-1]
    cols = slice(hd * hdim, (hd + 1) * hdim)

    xl = z_ref[:, cols]
    buf = jnp.concatenate([conv_hist[:, cols], xl], axis=0)
    cw = cw_ref[:, cols]
    xc = cb_ref[:, cols] + cw[CONV_WIDTH - 1:CONV_WIDTH] * xl
    for k in range(CONV_WIDTH - 1):
        back = CONV_WIDTH - 1 - k
        xc = xc + cw[k:k + 1] * pltpu.roll(buf, back, 0)[SUBLANES:]
    conv_hist[:, cols] = xl[ts - SUBLANES:]

    xcb = xc.astype(BF16)
    gate_a = jax.nn.sigmoid(
        jnp.dot(xcb, wa_ref[hd], preferred_element_type=F32) + ba_ref[:, cols])
    gate_x = jax.nn.sigmoid(
        jnp.dot(xcb, wx_ref[hd], preferred_element_type=F32) + bx_ref[:, cols])
    log_a = (-LRU_C) * gate_a * jax.nn.softplus(-lam_ref[:, cols])
    a = jnp.exp(log_a)
    mult = jnp.sqrt(_neg_expm1_nonpos(2.0 * log_a, a * a))
    u = (xc * gate_x) * mult
    hs = _lru_scan(a, u, h_carry, cols, ts)
    gl = z_ref[:, d_lru + hd * hdim:d_lru + (hd + 1) * hdim]
    y_ref[:, cols] = (hs * jax.nn.gelu(gl)).astype(y_ref.dtype)


def _pool_group(z_ref, small, y_ref, pool_hist, gi, t, ts):
    d_lru = small[0].shape[-1]
    pw_ref, pb_ref, ps_ref = small[7:]
    gd = pw_ref.shape[-1]
    w = POOL_WINDOWS[gi]
    cols = slice(gi * gd, (gi + 1) * gd)
    row = lax.broadcasted_iota(jnp.int32, (ts, 1), 0)
    xp = z_ref[:, 2 * d_lru + gi * gd:2 * d_lru + (gi + 1) * gd]
    s = jnp.concatenate([pool_hist[:, cols], xp], axis=0)
    pool_hist[:, cols] = xp[ts - POOL_HIST:]
    sh = 1
    while sh < w:
        s = s + pltpu.roll(s, sh, 0)
        sh *= 2
    cnt = jnp.minimum(t * ts + row + 1, w).astype(F32)
    p = (s[POOL_HIST:] / cnt - xp).astype(BF16)
    yp = (jnp.dot(p, pw_ref[gi], preferred_element_type=F32) + pb_ref[:, cols]) * ps_ref[:, cols]
    y_ref[:, d_lru + gi * gd:d_lru + (gi + 1) * gd] = yp.astype(y_ref.dtype)


def _mixer_kernel(x_ref, shift_ref, scale_ref, w_ref, *rest, ts, nt):
    small = rest[:10]
    y_ref, z_even, z_odd, h_mod, conv_hist, pool_hist, h_carry = rest[10:]
    t = pl.program_id(1)
    n_parts = LRU_HEADS
    assert len(POOL_WINDOWS) == n_parts
    slab = w_ref.shape[1] // n_parts

    def modulate():
        h_mod[...] = (x_ref[...] * (1.0 + scale_ref[...]) + shift_ref[...]).astype(BF16)

    def part(i, z_dst, z_src):
        if z_src is not None:
            _lru_head(z_src, small, y_ref, conv_hist, h_carry, i, ts)
            _pool_group(z_src, small, y_ref, pool_hist, i, t - 1, ts)
        if z_dst is not None:
            cols = slice(i * slab, (i + 1) * slab)
            z_dst[:, cols] = jnp.dot(h_mod[...], w_ref[:, cols], preferred_element_type=F32)

    def parts(z_dst, z_src):
        for i in range(n_parts):
            part(i, z_dst, z_src)

    @pl.when(t == 0)
    def _():
        conv_hist[...] = jnp.zeros_like(conv_hist)
        pool_hist[...] = jnp.zeros_like(pool_hist)
        h_carry[...] = jnp.zeros_like(h_carry)
        modulate()
        parts(z_even, None)

    inner = jnp.logical_and(t > 0, t < nt)

    @pl.when(jnp.logical_and(inner, t % 2 == 1))
    def _():
        modulate()
        parts(z_odd, z_even)

    @pl.when(jnp.logical_and(inner, t % 2 == 0))
    def _():
        modulate()
        parts(z_even, z_odd)

    @pl.when(t == nt)
    def _():
        parts(None, z_odd if (nt - 1) % 2 == 1 else z_even)


def _mixer(x2, mods, l, w_in_bf16, small, nb, seq):
    n, d = x2.shape
    d_in = w_in_bf16.shape[-1]
    d_lru = small[0].shape[-1]
    assert d_in == 3 * d_lru
    ts = 256
    nt = seq // ts
    batch_of = lambda b, t: b
    return pl.pallas_call(
        functools.partial(_mixer_kernel, ts=ts, nt=nt),
        grid=(nb, nt + 1),
        in_specs=[
            pl.BlockSpec((ts, d), lambda b, t: (b * nt + jnp.minimum(t, nt - 1), 0)),
            _mod_spec(mods, l, 0, SHIFT, batch_of),
            _mod_spec(mods, l, 0, SCALE, batch_of),
            pl.BlockSpec((None, d, d_in), lambda b, t: (l, 0, 0), pipeline_mode=pl.Buffered(1)),
        ] + [_layer_spec(a, l) for a in small],
        out_specs=pl.BlockSpec((ts, 2 * d_lru), lambda b, t: (b * nt + jnp.maximum(t - 1, 0), 0)),
        out_shape=jax.ShapeDtypeStruct((n, 2 * d_lru), BF16),
        scratch_shapes=[pltpu.VMEM((ts, d_in), F32), pltpu.VMEM((ts, d_in), F32),
                        pltpu.VMEM((ts, d), BF16),
                        pltpu.VMEM((SUBLANES, d_lru), F32),
                        pltpu.VMEM((POOL_HIST, d_lru), F32),
                        pltpu.VMEM((SUBLANES, d_lru), F32)],
        compiler_params=_params(("arbitrary", "arbitrary")),
        name="mixer",
    )(x2, mods, mods, w_in_bf16, *small)


def _resid_ln(x, o, gate, g, b, alpha):
    v = alpha * x + (1.0 + gate) * o
    mu = jnp.mean(v, axis=-1, keepdims=True)
    dv = v - mu
    var = jnp.mean(dv * dv, axis=-1, keepdims=True)
    return dv * lax.rsqrt(var + LN_EPS) * g + b


def _route(h, wr_ref):
    tm = h.shape[0]
    h_hi = h.astype(BF16)
    h_lo = (h - h_hi.astype(F32)).astype(BF16)
    r = jnp.dot(jnp.concatenate([h_hi, h_lo], axis=0), wr_ref[...], preferred_element_type=F32)
    r = r[:tm] + r[tm:]
    logits = r + pltpu.roll(r, LANES - N_EXPERTS, 1)
    lane = lax.broadcasted_iota(jnp.int32, logits.shape, 1).astype(F32)
    neg = jnp.float32(-jnp.inf)
    lg = jnp.where(lane < N_EXPERTS, logits, neg)
    m1 = jnp.max(lg, axis=-1, keepdims=True)
    i1 = jnp.min(jnp.where(lg == m1, lane, float(LANES)), axis=-1, keepdims=True)
    lg2 = jnp.where(lane == i1, neg, lg)
    m2 = jnp.max(lg2, axis=-1, keepdims=True)
    i2 = jnp.min(jnp.where(lg2 == m2, lane, float(LANES)), axis=-1, keepdims=True)
    e = jnp.exp(m2 - m1)
    p1 = 1.0 / (1.0 + e)
    p2 = e / (1.0 + e)
    idx = jnp.where(lane == 0, i1, i2).astype(jnp.int32)
    return idx, jnp.where(lane == 0, p1, p2)


def _router_weights(router_w):
    hi = router_w.astype(BF16)
    lo = (router_w - hi.astype(F32)).astype(BF16)
    wr = jnp.concatenate([hi, lo], axis=1)
    return jnp.pad(wr, ((0, 0), (0, LANES - wr.shape[1])))


def _mix_out_kernel(y_ref, x_ref, gate_ref, g_ref, b_ref, shift_ref, scale_ref, w_ref, *rest,
                    alpha, route):
    if route:
        wr_ref, xo_ref, h_ref, idx_ref, p_ref, w_scr = rest
    else:
        xo_ref, h_ref, w_scr = rest

    @pl.when(pl.program_id(0) == 0)
    def _():
        w_scr[...] = w_ref[...].astype(BF16)

    o = jnp.dot(y_ref[...], w_scr[...], preferred_element_type=F32)
    xn = _resid_ln(x_ref[...], o, gate_ref[...], g_ref[...], b_ref[...], alpha)
    xo_ref[...] = xn
    h = xn * (1.0 + scale_ref[...]) + shift_ref[...]
    h_ref[...] = h.astype(h_ref.dtype)
    if route:
        idx, p = _route(h, wr_ref)
        idx_ref[...] = idx
        p_ref[...] = p


def _mix_out(y, x2, mods, l, ln_g, ln_b, w_out_all, router_w, seq, alpha):
    n, d = x2.shape
    route = router_w is not None
    tm = 256
    per_b = seq // tm
    batch_of = lambda i: i // per_b
    tok = lambda: pl.BlockSpec((tm, d), lambda i: (i, 0))
    vec = lambda k: pl.BlockSpec((None, None, 1, d), lambda i: (l, k, 0, 0))
    ln_g4 = ln_g.reshape(ln_g.shape[0], 2, 1, d)
    ln_b4 = ln_b.reshape(ln_b.shape[0], 2, 1, d)
    in_specs = [tok(), tok(), _mod_spec(mods, l, 0, GATE, batch_of), vec(0), vec(0),
                _mod_spec(mods, l, 1, SHIFT, batch_of), _mod_spec(mods, l, 1, SCALE, batch_of),
                pl.BlockSpec((None, d, d), lambda i: (l, 0, 0), pipeline_mode=pl.Buffered(1))]
    args = [y, x2, mods, ln_g4, ln_b4, mods, mods, w_out_all]
    out_specs = [tok(), tok()]
    out_shape = [jax.ShapeDtypeStruct((n, d), F32),
                 jax.ShapeDtypeStruct((n, d), F32 if route else BF16)]
    if route:
        in_specs.append(pl.BlockSpec((d, LANES), lambda i: (0, 0)))
        args.append(_router_weights(router_w))
        out_specs += [pl.BlockSpec((tm, LANES), lambda i: (i, 0))] * 2
        out_shape += [jax.ShapeDtypeStruct((n, LANES), jnp.int32),
                      jax.ShapeDtypeStruct((n, LANES), F32)]
    return pl.pallas_call(
        functools.partial(_mix_out_kernel, alpha=alpha, route=route),
        grid=(n // tm,),
        in_specs=in_specs,
        out_specs=out_specs,
        out_shape=out_shape,
        scratch_shapes=[pltpu.VMEM((d, d), BF16)],
        compiler_params=_params(("arbitrary",)),
        name="mix_out",
    )(*args)


def _ffn_kernel(ge_ref, gc_ref, gb_ref, gv_ref, x_ref, wg0_ref, wu0_ref, wd0_ref, wgn_ref, wun_ref,
                wdn_ref, *rest, ch, alpha, ln, static_n):
    if ln:
        xres_hbm, gate_ref, g_ref, b_ref, o_ref = rest[:5]
        bufs, (xr_buf, xr_sem) = rest[5:11], rest[11:]
    else:
        o_ref, bufs = rest[0], rest[1:7]
    w_bf16 = (bufs[:3], bufs[3:])
    s = pl.program_id(0)
    j = pl.program_id(1)
    nj = pl.num_programs(1)
    n = gc_ref[s]
    cap = x_ref.shape[0]

    @pl.when(j == 0)
    def _():
        o_ref[...] = jnp.zeros_like(o_ref)

    @pl.when(jnp.logical_and(s == 0, j == 0))
    def _():
        for dst, src in zip(w_bf16[0], (wg0_ref, wu0_ref, wd0_ref)):
            dst[...] = src[...].astype(BF16)

    def chunk_rows(c):
        return pl.ds(c * ch if isinstance(c, int) else pl.multiple_of(c * ch, ch), ch)

    def step(parity, static_k):
        wg_s, wu_s, wd_s = w_bf16[parity]

        def up(c):
            x = x_ref[chunk_rows(c), :]
            g = jnp.dot(x, wg_s[...], preferred_element_type=F32)
            u = jnp.dot(x, wu_s[...], preferred_element_type=F32)
            return (g * jax.nn.sigmoid(g) * u).astype(BF16)

        def down(c, hmid):
            o_ref[chunk_rows(c), :] += jnp.dot(hmid, wd_s[...], preferred_element_type=F32)

        def fused(c, hmid):
            down(c - 1, hmid)
            return up(c)

        def convert_next(piece, n_pieces):
            for dst, src in zip(w_bf16[1 - parity], (wgn_ref, wun_ref, wdn_ref)):
                rows = src.shape[0] // n_pieces
                sl = slice(piece * rows, (piece + 1) * rows)
                dst[sl, :] = src[sl, :].astype(BF16)

        if static_k is not None and static_k >= 3:
            n_pieces = min(4, static_k - 2)
            pending = [up(0), up(1)]
            for c in range(2, static_k):
                down(c - 2, pending.pop(0))
                pending.append(up(c))
                if c - 2 < n_pieces:
                    convert_next(c - 2, n_pieces)
            down(static_k - 2, pending.pop(0))
            down(static_k - 1, pending.pop(0))
            return

        hmid = up(0)
        if static_k is not None:
            convert_next(0, 1)
            for c in range(1, static_k):
                hmid = fused(c, hmid)
            down(static_k - 1, hmid)
        else:
            convert_next(0, 1)
            pairs = lax.div(n - 1, 2)

            def two(p, hmid):
                return fused(2 * p + 2, fused(2 * p + 1, hmid))

            hmid = lax.fori_loop(0, pairs, two, hmid)
            hmid = lax.fori_loop(2 * pairs + 1, n, fused, hmid)
            down(n - 1, hmid)

    is_static = functools.reduce(jnp.logical_or, [n == k for k in static_n], n < 0)
    for parity in (0, 1):
        mine = j % 2 == parity
        for k in static_n:
            pl.when(jnp.logical_and(mine, n == k))(functools.partial(step, parity, k))
        pl.when(jnp.logical_and(mine, jnp.logical_and(n > 0, jnp.logical_not(is_static))))(
            functools.partial(step, parity, None))

    if ln:
        row0 = gb_ref[s] * cap

        def xres_copy(c, slot):
            return pltpu.make_async_copy(
                xres_hbm.at[pl.ds(row0 + c * ch, ch)], xr_buf.at[slot], xr_sem.at[slot])

        @pl.when(jnp.logical_and(j == nj - 1, n > 0))
        def _():
            xres_copy(0, 0).start()

            def fin(c, carry):
                slot = c % 2

                @pl.when(c + 1 < n)
                def _():
                    xres_copy(c + 1, 1 - slot).start()

                xres_copy(c, slot).wait()
                rows = pl.ds(pl.multiple_of(c * ch, ch), ch)
                o_ref[rows, :] = _resid_ln(xr_buf[slot], o_ref[rows, :],
                                           gate_ref[...], g_ref[...], b_ref[...], alpha)
                return carry

            lax.fori_loop(0, n, fin, 0)


def _ffn(x_rows, w_gate, w_up, w_down, group_expert, group_chunks, group_block, groups_used, *,
         cap, tf, ch, static_n, ln_args=None, seq=None, alpha=None):
    p_rows, d = x_rows.shape
    f = w_gate.shape[-1]
    n_groups = group_expert.shape[0]
    nj = f // tf
    assert nj % 2 == 0
    ln = ln_args is not None

    def following(s, j, ge, gc, gv):
        wrap = j + 1 >= nj
        s2 = jnp.where(wrap, s + 1, s)
        j2 = jnp.where(wrap, 0, j + 1)
        live = s2 < gv[0]
        last = gv[0] - 1
        return ge[jnp.where(live, s2, last)], jnp.where(live, j2, nj - 1)

    def first_up(s, j, ge, gc, gb, gv):
        return ge[0], 0, 0

    def first_down(s, j, ge, gc, gb, gv):
        return ge[0], 0, 0

    def next_up(s, j, ge, gc, gb, gv):
        e, j2 = following(s, j, ge, gc, gv)
        return e, 0, j2

    def next_down(s, j, ge, gc, gb, gv):
        e, j2 = following(s, j, ge, gc, gv)
        return e, j2, 0

    once = pl.Buffered(1)
    in_specs = [
        pl.BlockSpec((cap, d), lambda s, j, ge, gc, gb, gv: (gb[s], 0), pipeline_mode=once),
        pl.BlockSpec((None, d, tf), first_up, pipeline_mode=once),
        pl.BlockSpec((None, d, tf), first_up, pipeline_mode=once),
        pl.BlockSpec((None, tf, d), first_down, pipeline_mode=once),
        pl.BlockSpec((None, d, tf), next_up),
        pl.BlockSpec((None, d, tf), next_up),
        pl.BlockSpec((None, tf, d), next_down),
    ]
    args = [x_rows, w_gate, w_up, w_down, w_gate, w_up, w_down]
    scratch = 2 * [pltpu.VMEM((d, tf), BF16), pltpu.VMEM((d, tf), BF16), pltpu.VMEM((tf, d), BF16)]
    if ln:
        xres, mods, l, ln_g, ln_b = ln_args
        assert seq % cap == 0
        per_b = seq // cap
        ln_g4 = ln_g.reshape(ln_g.shape[0], 2, 1, d)
        ln_b4 = ln_b.reshape(ln_b.shape[0], 2, 1, d)
        vec = lambda: pl.BlockSpec((None, None, 1, d), lambda s, j, ge, gc, gb, gv: (l, 1, 0, 0))
        in_specs += [
            pl.BlockSpec(memory_space=pl.ANY),
            _mod_spec(mods, l, 1, GATE, lambda s, j, ge, gc, gb, gv: gb[s] // per_b),
            vec(), vec(),
        ]
        args += [xres, mods, ln_g4, ln_b4]
        scratch += [pltpu.VMEM((2, ch, d), F32), pltpu.SemaphoreType.DMA((2,))]
    grid_spec = pltpu.PrefetchScalarGridSpec(
        num_scalar_prefetch=4,
        grid=(n_groups, nj),
        in_specs=in_specs,
        out_specs=pl.BlockSpec((cap, d), lambda s, j, ge, gc, gb, gv: (s, 0), pipeline_mode=once),
        scratch_shapes=scratch,
    )
    return pl.pallas_call(
        functools.partial(_ffn_kernel, ch=ch, alpha=alpha, ln=ln, static_n=static_n),
        grid_spec=grid_spec,
        out_shape=jax.ShapeDtypeStruct((p_rows, d), F32),
        compiler_params=_params(("arbitrary", "arbitrary")),
        name="ffn_ln" if ln else "ffn_moe",
    )(group_expert, group_chunks, group_block, groups_used, *args)


def _row_copy(src_hbm, src_row, dst_buf, dst_row, sem):
    return pltpu.make_async_copy(src_hbm.at[pl.ds(src_row, 1)], dst_buf.at[pl.ds(dst_row, 1)], sem)


def _next_tile_spec(n_tiles, width):
    return pl.BlockSpec((None, 1, width), lambda i, *_: (jnp.minimum(i + 1, n_tiles - 1), 0, 0),
                        memory_space=pltpu.SMEM)


def _first_tile_spec(width):
    return pl.BlockSpec((None, 1, width), lambda i, *_: (0, 0, 0), memory_space=pltpu.SMEM)


def _gather_kernel(cnt_ref, tok0_ref, tokn_ref, h_hbm, o_ref, buf, sem, *, tg):
    i = pl.program_id(0)
    n = pl.num_programs(0)

    def bursts(tile):
        return lax.div(cnt_ref[tile], ROW_BURST)

    def issue(tok_ref, tile, slot):
        def body(g, carry):
            for k in range(ROW_BURST):
                r = g * ROW_BURST + k
                _row_copy(h_hbm, tok_ref[0, r], buf.at[slot], r, sem.at[slot]).start()
            return carry
        lax.fori_loop(0, bursts(tile), body, 0)

    def drain(tile, slot):
        def body(g, carry):
            for k in range(ROW_BURST):
                _row_copy(h_hbm, 0, buf.at[slot], g * ROW_BURST + k, sem.at[slot]).wait()
            return carry
        lax.fori_loop(0, bursts(tile), body, 0)

    @pl.when(i == 0)
    def _():
        buf[...] = jnp.zeros_like(buf)
        issue(tok0_ref, 0, 0)

    @pl.when(i + 1 < n)
    def _():
        issue(tokn_ref, i + 1, (i + 1) % 2)

    drain(i, i % 2)
    o_ref[...] = buf[i % 2].astype(o_ref.dtype)


def _gather(h, tok_of_row, tile_rows, p_rows):
    n, d = h.shape
    tg = GATHER_TILE
    n_tiles = p_rows // tg
    tok3 = tok_of_row.reshape(n_tiles, 1, tg)
    grid_spec = pltpu.PrefetchScalarGridSpec(
        num_scalar_prefetch=1,
        grid=(n_tiles,),
        in_specs=[_first_tile_spec(tg), _next_tile_spec(n_tiles, tg),
                  pl.BlockSpec(memory_space=pl.ANY)],
        out_specs=pl.BlockSpec((tg, d), lambda i, cnt: (i, 0)),
        scratch_shapes=[pltpu.VMEM((2, tg, d), F32), pltpu.SemaphoreType.DMA((2,))],
    )
    return pl.pallas_call(
        functools.partial(_gather_kernel, tg=tg),
        grid_spec=grid_spec,
        out_shape=jax.ShapeDtypeStruct((p_rows, d), BF16),
        compiler_params=_params(("arbitrary",)),
        name="gather",
    )(tile_rows, tok3, tok3, h)


def _combine_kernel(dst0_ref, dstn_ref, x_ref, p_ref, gate_ref, g_ref, b_ref, ys_hbm, o_ref,
                    buf, sem, *, tc, alpha):
    i = pl.program_id(0)
    n = pl.num_programs(0)

    def issue(dst_ref, slot):
        def body(r, carry):
            for k in range(TOP_K):
                _row_copy(ys_hbm, dst_ref[0, r * TOP_K + k], buf.at[slot, k], r,
                          sem.at[slot]).start()
            return carry
        lax.fori_loop(0, tc, body, 0, unroll=4)

    def drain(slot):
        def body(r, carry):
            for k in range(TOP_K):
                _row_copy(ys_hbm, 0, buf.at[slot, k], r, sem.at[slot]).wait()
            return carry
        lax.fori_loop(0, tc, body, 0, unroll=4)

    @pl.when(i == 0)
    def _():
        issue(dst0_ref, 0)

    @pl.when(i + 1 < n)
    def _():
        issue(dstn_ref, (i + 1) % 2)

    slot = i % 2
    drain(slot)
    p = p_ref[...]
    o = p[:, 0:1] * buf[slot, 0] + p[:, 1:2] * buf[slot, 1]
    o_ref[...] = _resid_ln(x_ref[...], o, gate_ref[...], g_ref[...], b_ref[...], alpha)


def _combine(ys, dest, x2, probs, mods, l, ln_g, ln_b, seq, alpha):
    n, d = x2.shape
    tc = 256
    n_tiles = n // tc
    per_b = seq // tc
    dest3 = dest.reshape(n_tiles, 1, tc * TOP_K)
    ln_g4 = ln_g.reshape(ln_g.shape[0], 2, 1, d)
    ln_b4 = ln_b.reshape(ln_b.shape[0], 2, 1, d)
    vec = lambda: pl.BlockSpec((None, None, 1, d), lambda i: (l, 1, 0, 0))
    return pl.pallas_call(
        functools.partial(_combine_kernel, tc=tc, alpha=alpha),
        grid=(n_tiles,),
        in_specs=[
            _first_tile_spec(tc * TOP_K), _next_tile_spec(n_tiles, tc * TOP_K),
            pl.BlockSpec((tc, d), lambda i: (i, 0)),
            pl.BlockSpec((tc, LANES), lambda i: (i, 0)),
            _mod_spec(mods, l, 1, GATE, lambda i: i // per_b),
            vec(), vec(),
            pl.BlockSpec(memory_space=pl.ANY),
        ],
        out_specs=pl.BlockSpec((tc, d), lambda i: (i, 0)),
        out_shape=jax.ShapeDtypeStruct((n, d), F32),
        scratch_shapes=[pltpu.VMEM((2, TOP_K, tc, d), F32), pltpu.SemaphoreType.DMA((2,))],
        compiler_params=_params(("arbitrary",)),
        name="combine",
    )(dest3, dest3, x2, probs, mods, ln_g4, ln_b4, ys)


def _routing_tables(top_i, cap, ch, max_groups):
    n = top_i.shape[0]
    e_flat = top_i.reshape(-1)
    onehot = (e_flat[:, None] == jnp.arange(N_EXPERTS, dtype=jnp.int32)[None, :]).astype(jnp.int32)
    csum = jnp.cumsum(onehot, axis=0)
    counts = csum[-1]
    rank = jnp.take_along_axis(csum, e_flat[:, None], axis=1)[:, 0] - 1
    groups_per = (counts + cap - 1) // cap
    group_end = jnp.cumsum(groups_per)
    group_start = group_end - groups_per
    dest = (group_start * cap)[e_flat] + rank
    n_groups = group_end[-1]
    gidx = jnp.arange(max_groups, dtype=jnp.int32)
    gclamp = jnp.minimum(gidx, n_groups - 1)
    group_expert = jnp.sum((group_end[None, :] <= gclamp[:, None]).astype(jnp.int32), axis=1)
    rows = jnp.clip(counts[group_expert] - (gclamp - group_start[group_expert]) * cap, 0, cap)
    group_chunks = jnp.where(gidx < n_groups, (rows + ch - 1) // ch, 0)
    p_rows = max_groups * cap
    tok_of_row = (jnp.arange(p_rows, dtype=jnp.int32) % n).at[dest].set(
        jnp.arange(n * TOP_K, dtype=jnp.int32) // TOP_K)
    tiles_per_group = cap // GATHER_TILE
    in_group = jnp.where(gidx < n_groups, rows, 0)[:, None] - (
        jnp.arange(tiles_per_group, dtype=jnp.int32) * GATHER_TILE)[None, :]
    tile_rows = jnp.clip(in_group, 0, GATHER_TILE).reshape(-1)
    tile_rows = (tile_rows + ROW_BURST - 1) // ROW_BURST * ROW_BURST
    return (dest.astype(jnp.int32), tok_of_row, tile_rows.astype(jnp.int32),
            group_expert.astype(jnp.int32), group_chunks.astype(jnp.int32), gclamp.astype(jnp.int32),
            n_groups.reshape(1).astype(jnp.int32))


def kernel(x, c, ada_w, ada_b, ln_g, ln_b, mix_w_in, conv_w, conv_b, lru_wa, lru_ba, lru_wx, lru_bx,
           lru_lam, pool_w, pool_b, pool_scale, mix_w_out, ffn_w_gate, ffn_w_up, ffn_w_down,
           router_w, exp_w_gate, exp_w_up, exp_w_down):
    nb, seq, d = x.shape
    depth = ada_w.shape[0]
    n = nb * seq
    alpha = float((2 * depth) ** 0.25)
    mods = _ada(c, ada_w, ada_b)
    x2 = x.reshape(n, d)

    row3 = lambda v: v.reshape(depth, 1, -1)
    seq_params = [conv_w, row3(conv_b), lru_wa.astype(BF16), row3(lru_ba), lru_wx.astype(BF16),
                  row3(lru_bx), row3(lru_lam), pool_w.astype(BF16), row3(pool_b), row3(pool_scale)]

    w_in_bf16 = _to_bf16(mix_w_in)

    ffn_tf, ffn_ch = 256, 256
    dense_cap = seq
    moe_cap = 2304
    for l in range(depth):
        moe = (l % 2 == 1)
        i = l // 2

        y = _mixer(x2, mods, l, w_in_bf16, seq_params, nb, seq)
        outs = _mix_out(y, x2, mods, l, ln_g, ln_b, mix_w_out, router_w[i] if moe else None,
                        seq, alpha)
        if not moe:
            x2, h = outs
            n_groups = n // dense_cap
            ge = jnp.full((n_groups,), i, jnp.int32)
            gc = jnp.full((n_groups,), dense_cap // ffn_ch, jnp.int32)
            gb = jnp.arange(n_groups, dtype=jnp.int32)
            gv = jnp.full((1,), n_groups, jnp.int32)
            x2 = _ffn(h, ffn_w_gate, ffn_w_up, ffn_w_down, ge, gc, gb, gv,
                      cap=dense_cap, tf=ffn_tf, ch=ffn_ch, static_n=(dense_cap // ffn_ch,),
                      ln_args=(x2, mods, l, ln_g, ln_b), seq=seq, alpha=alpha)
        else:
            x_mix, h, idx, probs = outs
            top_i = idx[:, :TOP_K]
            n_exp = exp_w_gate.shape[1]
            f_exp = exp_w_gate.shape[-1]

            def moe(max_groups):
                dest, tok_of_row, tile_rows, ge, gc, gb, gv = _routing_tables(
                    top_i, moe_cap, ffn_ch, max_groups)
                xs = _gather(h, tok_of_row, tile_rows, max_groups * moe_cap)
                ys = _ffn(xs,
                          exp_w_gate.reshape(-1, d, f_exp), exp_w_up.reshape(-1, d, f_exp),
                          exp_w_down.reshape(-1, f_exp, d), ge + i * n_exp, gc, gb, gv,
                          cap=moe_cap, tf=ffn_tf, ch=ffn_ch,
                          static_n=(moe_cap // ffn_ch - 1, moe_cap // ffn_ch))
                return _combine(ys, dest, x_mix, probs, mods, l, ln_g, ln_b, seq, alpha)

            counts = jnp.sum(top_i.reshape(-1)[:, None] == jnp.arange(n_exp)[None, :], axis=0)
            x2 = lax.cond(jnp.max(counts) <= moe_cap,
                          functools.partial(moe, n_exp),
                          functools.partial(moe, (n * TOP_K) // moe_cap + n_exp))
    return x2.reshape(nb, seq, d)
```

```python
import functools

import jax
import jax.numpy as jnp
from jax import lax
from jax.experimental import pallas as pl
from jax.experimental.pallas import tpu as pltpu

F32 = jnp.float32
BF16 = jnp.bfloat16

LN_EPS = 1e-5
LRU_C = 8.0
CONV_WIDTH = 4
LRU_HEADS = 4
POOL_WINDOWS = (2, 4, 8, 16)
N_EXPERTS = 8
TOP_K = 2

V7X_VMEM_LIMIT_BYTES = 58 * 1024 * 1024
LANES = 128
SUBLANES = 8
POOL_HIST = 16
SHIFT, SCALE, GATE = 0, 1, 2
GATHER_TILE = 256
ROW_BURST = 8


def _params(sem, vmem=V7X_VMEM_LIMIT_BYTES):
    return pltpu.CompilerParams(dimension_semantics=sem, vmem_limit_bytes=vmem)


def _mod_spec(mods, l, k, which, batch_of):
    d = mods.shape[-1]
    return pl.BlockSpec((None, None, None, None, 1, d),
                        lambda *g: (l, k, batch_of(*g), which, 0, 0))


def _layer_spec(arr, l):
    nd = arr.ndim - 1
    return pl.BlockSpec((None,) + arr.shape[1:], lambda *g: (l,) + (0,) * nd)


def _ada_kernel(c_ref, w_ref, b_ref, o_ref):
    c = c_ref[...]
    c_act = (c * jax.nn.sigmoid(c)).astype(BF16)
    o_ref[...] = jnp.dot(c_act, w_ref[...].astype(BF16),
                         preferred_element_type=F32) + b_ref[...]


def _ada(c, ada_w, ada_b):
    depth, _, d, d3 = ada_w.shape
    nb = c.shape[0]
    rows = -(-nb // SUBLANES) * SUBLANES
    c_pad = jnp.pad(c, ((0, rows - nb), (0, 0)))
    w = ada_w.reshape(depth * 2, d, d3)
    b = ada_b.reshape(depth * 2, 1, d3)
    tn = 1024
    out = pl.pallas_call(
        _ada_kernel,
        grid=(depth * 2, d3 // tn),
        in_specs=[
            pl.BlockSpec((rows, d), lambda l, j: (0, 0)),
            pl.BlockSpec((None, d, tn), lambda l, j: (l, 0, j)),
            pl.BlockSpec((None, 1, tn), lambda l, j: (l, 0, j)),
        ],
        out_specs=pl.BlockSpec((None, rows, tn), lambda l, j: (l, 0, j)),
        out_shape=jax.ShapeDtypeStruct((depth * 2, rows, d3), F32),
        compiler_params=_params(("arbitrary", "arbitrary")),
        name="ada",
    )(c_pad, w, b)
    return out.reshape(depth, 2, rows, 3, 1, d)


def _to_bf16_kernel(w_ref, o_ref):
    o_ref[...] = w_ref[...].astype(BF16)


def _to_bf16(w):
    shape = w.shape
    w2 = w.reshape(-1, shape[-1])
    tr = 512
    out = pl.pallas_call(
        _to_bf16_kernel,
        grid=(w2.shape[0] // tr,),
        in_specs=[pl.BlockSpec((tr, shape[-1]), lambda i: (i, 0))],
        out_specs=pl.BlockSpec((tr, shape[-1]), lambda i: (i, 0)),
        out_shape=jax.ShapeDtypeStruct(w2.shape, BF16),
        compiler_params=_params(("arbitrary",)),
        name="to_bf16",
    )(w2)
    return out.reshape(shape)


def _neg_expm1_nonpos(v, exp_v):
    poly = 1.0 / 120.0
    for coef in (1.0 / 24.0, 1.0 / 6.0, 0.5, 1.0):
        poly = poly * v + coef
    return jnp.where(v > -0.01, -v * poly, 1.0 - exp_v)


def _lru_scan(a, u, h_carry, cols, ts):
    row = lax.broadcasted_iota(jnp.int32, (ts, 1), 0) % SUBLANES
    d = 1
    while d < SUBLANES:
        keep = row >= d
        a_prev = jnp.where(keep, pltpu.roll(a, d, 0), 1.0)
        u_prev = jnp.where(keep, pltpu.roll(u, d, 0), 0.0)
        u = u + a * u_prev
        a = a * a_prev
        d *= 2
    carry = h_carry[0:1, cols]
    groups = []
    for g in range(ts // SUBLANES):
        rows = slice(g * SUBLANES, (g + 1) * SUBLANES)
        blk = a[rows] * carry + u[rows]
        groups.append(blk)
        carry = blk[SUBLANES - 1:SUBLANES]
    h_carry[:, cols] = jnp.broadcast_to(carry, (h_carry.shape[0], carry.shape[1]))
    return jnp.concatenate(groups, axis=0)


def _lru_head(z_ref, small, y_ref, conv_buf, h_carry, hd, ts):
    cw_ref, cb_ref, wa_ref, ba_ref, wx_ref, bx_ref, lam_ref = small[:7]
    d_lru = cw_ref.shape[-1]
    hdim = wa_ref.shape[-1]
    cols = slice(hd * hdim, (hd + 1) * hdim)

    xl = z_ref[:, cols]
    conv_buf[SUBLANES:, cols] = xl
    cw = cw_ref[:, cols]
    xc = cb_ref[:, cols] + cw[CONV_WIDTH - 1:CONV_WIDTH] * xl
    for k in range(CONV_WIDTH - 1):
        back = CONV_WIDTH - 1 - k
        xc = xc + cw[k:k + 1] * conv_buf[SUBLANES - back:SUBLANES - back + ts, cols]
    conv_buf[:SUBLANES, cols] = xl[ts - SUBLANES:]

    xcb = xc.astype(BF16)
    gate_a = jax.nn.sigmoid(
        jnp.dot(xcb, wa_ref[hd], preferred_element_type=F32) + ba_ref[:, cols])
    gate_x = jax.nn.sigmoid(
        jnp.dot(xcb, wx_ref[hd], preferred_element_type=F32) + bx_ref[:, cols])
    log_a = (-LRU_C) * gate_a * jax.nn.softplus(-lam_ref[:, cols])
    a = jnp.exp(log_a)
    m2 = _neg_expm1_nonpos(2.0 * log_a, a * a)
    mult = jnp.where(m2 > 0.0, m2 * lax.rsqrt(m2), 0.0)
    u = (xc * gate_x) * mult
    hs = _lru_scan(a, u, h_carry, cols, ts)
    gl = z_ref[:, d_lru + hd * hdim:d_lru + (hd + 1) * hdim]
    y_ref[:, cols] = (hs * jax.nn.gelu(gl)).astype(y_ref.dtype)


def _pool_group(z_ref, small, y_ref, pool_hist, gi, t, ts):
    d_lru = small[0].shape[-1]
    pw_ref, pb_ref, ps_ref = small[7:]
    gd = pw_ref.shape[-1]
    w = POOL_WINDOWS[gi]
    cols = slice(gi * gd, (gi + 1) * gd)
    row = lax.broadcasted_iota(jnp.int32, (ts, 1), 0)
    xp = z_ref[:, 2 * d_lru + gi * gd:2 * d_lru + (gi + 1) * gd]
    s = jnp.concatenate([pool_hist[:, cols], xp], axis=0)
    pool_hist[:, cols] = xp[ts - POOL_HIST:]
    sh = 1
    while sh < w:
        s = s + pltpu.roll(s, sh, 0)
        sh *= 2
    cnt = jnp.minimum(t * ts + row + 1, w).astype(F32)
    p = (s[POOL_HIST:] / cnt - xp).astype(BF16)
    yp = (jnp.dot(p, pw_ref[gi], preferred_element_type=F32) + pb_ref[:, cols]) * ps_ref[:, cols]
    y_ref[:, d_lru + gi * gd:d_lru + (gi + 1) * gd] = yp.astype(y_ref.dtype)


def _mixer_kernel(x_ref, shift_ref, scale_ref, w_ref, *rest, ts, nt):
    small = rest[:10]
    y_ref, z_even, z_odd, h_mod, conv_buf, pool_hist, h_carry = rest[10:]
    t = pl.program_id(1)
    n_parts = LRU_HEADS
    assert len(POOL_WINDOWS) == n_parts
    slab = w_ref.shape[1] // n_parts

    def modulate():
        h_mod[...] = (x_ref[...] * (1.0 + scale_ref[...]) + shift_ref[...]).astype(BF16)

    def part(i, z_dst, z_src):
        if z_dst is not None:
            cols = slice(i * slab, (i + 1) * slab)
            z_dst[:, cols] = jnp.dot(h_mod[...], w_ref[:, cols], preferred_element_type=F32)
        if z_src is not None:
            _lru_head(z_src, small, y_ref, conv_buf, h_carry, i, ts)
            _pool_group(z_src, small, y_ref, pool_hist, i, t - 1, ts)

    def parts(z_dst, z_src):
        for i in range(n_parts):
            part(i, z_dst, z_src)

    @pl.when(t == 0)
    def _():
        conv_buf[...] = jnp.zeros_like(conv_buf)
        pool_hist[...] = jnp.zeros_like(pool_hist)
        h_carry[...] = jnp.zeros_like(h_carry)
        modulate()
        parts(z_even, None)

    inner = jnp.logical_and(t > 0, t < nt)

    @pl.when(jnp.logical_and(inner, t % 2 == 1))
    def _():
        modulate()
        parts(z_odd, z_even)

    @pl.when(jnp.logical_and(inner, t % 2 == 0))
    def _():
        modulate()
        parts(z_even, z_odd)

    @pl.when(t == nt)
    def _():
        parts(None, z_odd if (nt - 1) % 2 == 1 else z_even)


def _mixer(x2, mods, l, w_in_bf16, small, nb, seq):
    n, d = x2.shape
    d_in = w_in_bf16.shape[-1]
    d_lru = small[0].shape[-1]
    assert d_in == 3 * d_lru
    ts = 256
    nt = seq // ts
    batch_of = lambda b, t: b
    return pl.pallas_call(
        functools.partial(_mixer_kernel, ts=ts, nt=nt),
        grid=(nb, nt + 1),
        in_specs=[
            pl.BlockSpec((ts, d), lambda b, t: (b * nt + jnp.minimum(t, nt - 1), 0)),
            _mod_spec(mods, l, 0, SHIFT, batch_of),
            _mod_spec(mods, l, 0, SCALE, batch_of),
            pl.BlockSpec((None, d, d_in), lambda b, t: (l, 0, 0), pipeline_mode=pl.Buffered(1)),
        ] + [_layer_spec(a, l) for a in small],
        out_specs=pl.BlockSpec((ts, 2 * d_lru), lambda b, t: (b * nt + jnp.maximum(t - 1, 0), 0)),
        out_shape=jax.ShapeDtypeStruct((n, 2 * d_lru), BF16),
        scratch_shapes=[pltpu.VMEM((ts, d_in), F32), pltpu.VMEM((ts, d_in), F32),
                        pltpu.VMEM((ts, d), BF16),
                        pltpu.VMEM((SUBLANES + ts, d_lru), F32),
                        pltpu.VMEM((POOL_HIST, d_lru), F32),
                        pltpu.VMEM((SUBLANES, d_lru), F32)],
        compiler_params=_params(("arbitrary", "arbitrary")),
        name="mixer",
    )(x2, mods, mods, w_in_bf16, *small)


def _resid_ln(x, o, gate, g, b, alpha):
    v = alpha * x + (1.0 + gate) * o
    mu = jnp.mean(v, axis=-1, keepdims=True)
    dv = v - mu
    var = jnp.mean(dv * dv, axis=-1, keepdims=True)
    return dv * lax.rsqrt(var + LN_EPS) * g + b


def _route(h, wr_ref):
    tm = h.shape[0]
    h_hi = h.astype(BF16)
    h_lo = (h - h_hi.astype(F32)).astype(BF16)
    r = jnp.dot(jnp.concatenate([h_hi, h_lo], axis=0), wr_ref[...], preferred_element_type=F32)
    r = r[:tm] + r[tm:]
    logits = r + pltpu.roll(r, LANES - N_EXPERTS, 1)
    lane = lax.broadcasted_iota(jnp.int32, logits.shape, 1).astype(F32)
    neg = jnp.float32(-jnp.inf)
    lg = jnp.where(lane < N_EXPERTS, logits, neg)
    m1 = jnp.max(lg, axis=-1, keepdims=True)
    i1 = jnp.min(jnp.where(lg == m1, lane, float(LANES)), axis=-1, keepdims=True)
    lg2 = jnp.where(lane == i1, neg, lg)
    m2 = jnp.max(lg2, axis=-1, keepdims=True)
    i2 = jnp.min(jnp.where(lg2 == m2, lane, float(LANES)), axis=-1, keepdims=True)
    e = jnp.exp(m2 - m1)
    p1 = 1.0 / (1.0 + e)
    p2 = e / (1.0 + e)
    idx = jnp.where(lane == 0, i1, i2).astype(jnp.int32)
    return idx, jnp.where(lane == 0, p1, p2)


def _router_weights(router_w):
    hi = router_w.astype(BF16)
    lo = (router_w - hi.astype(F32)).astype(BF16)
    wr = jnp.concatenate([hi, lo], axis=1)
    return jnp.pad(wr, ((0, 0), (0, LANES - wr.shape[1])))


def _mix_out_kernel(y_ref, x_ref, gate_ref, g_ref, b_ref, shift_ref, scale_ref, w_ref, *rest,
                    alpha, route):
    if route:
        wr_ref, xo_ref, h_ref, idx_ref, p_ref, w_scr = rest
    else:
        xo_ref, h_ref, w_scr = rest

    @pl.when(pl.program_id(0) == 0)
    def _():
        w_scr[...] = w_ref[...].astype(BF16)

    o = jnp.dot(y_ref[...], w_scr[...], preferred_element_type=F32)
    xn = _resid_ln(x_ref[...], o, gate_ref[...], g_ref[...], b_ref[...], alpha)
    xo_ref[...] = xn
    h = xn * (1.0 + scale_ref[...]) + shift_ref[...]
    h_ref[...] = h.astype(h_ref.dtype)
    if route:
        idx, p = _route(h, wr_ref)
        idx_ref[...] = idx
        p_ref[...] = p


def _mix_out(y, x2, mods, l, ln_g, ln_b, w_out_all, router_w, seq, alpha):
    n, d = x2.shape
    route = router_w is not None
    tm = 256
    per_b = seq // tm
    batch_of = lambda i: i // per_b
    tok = lambda: pl.BlockSpec((tm, d), lambda i: (i, 0))
    vec = lambda k: pl.BlockSpec((None, None, 1, d), lambda i: (l, k, 0, 0))
    ln_g4 = ln_g.reshape(ln_g.shape[0], 2, 1, d)
    ln_b4 = ln_b.reshape(ln_b.shape[0], 2, 1, d)
    in_specs = [tok(), tok(), _mod_spec(mods, l, 0, GATE, batch_of), vec(0), vec(0),
                _mod_spec(mods, l, 1, SHIFT, batch_of), _mod_spec(mods, l, 1, SCALE, batch_of),
                pl.BlockSpec((None, d, d), lambda i: (l, 0, 0), pipeline_mode=pl.Buffered(1))]
    args = [y, x2, mods, ln_g4, ln_b4, mods, mods, w_out_all]
    out_specs = [tok(), tok()]
    out_shape = [jax.ShapeDtypeStruct((n, d), F32),
                 jax.ShapeDtypeStruct((n, d), F32 if route else BF16)]
    if route:
        in_specs.append(pl.BlockSpec((d, LANES), lambda i: (0, 0)))
        args.append(_router_weights(router_w))
        out_specs += [pl.BlockSpec((tm, LANES), lambda i: (i, 0))] * 2
        out_shape += [jax.ShapeDtypeStruct((n, LANES), jnp.int32),
                      jax.ShapeDtypeStruct((n, LANES), F32)]
    return pl.pallas_call(
        functools.partial(_mix_out_kernel, alpha=alpha, route=route),
        grid=(n // tm,),
        in_specs=in_specs,
        out_specs=out_specs,
        out_shape=out_shape,
        scratch_shapes=[pltpu.VMEM((d, d), BF16)],
        compiler_params=_params(("arbitrary",)),
        name="mix_out",
    )(*args)


def _ffn_kernel(ge_ref, gc_ref, gb_ref, gv_ref, x_ref, wg0_ref, wu0_ref, wd0_ref, wgn_ref, wun_ref,
                wdn_ref, *rest, ch, alpha, ln, static_n):
    if ln:
        xres_hbm, gate_ref, g_ref, b_ref, o_ref = rest[:5]
        bufs, (xr_buf, xr_sem) = rest[5:11], rest[11:]
    else:
        o_ref, bufs = rest[0], rest[1:7]
    w_bf16 = (bufs[:3], bufs[3:])
    s = pl.program_id(0)
    j = pl.program_id(1)
    nj = pl.num_programs(1)
    n = gc_ref[s]
    cap = x_ref.shape[0]

    @pl.when(j == 0)
    def _():
        o_ref[...] = jnp.zeros_like(o_ref)

    @pl.when(jnp.logical_and(s == 0, j == 0))
    def _():
        for dst, src in zip(w_bf16[0], (wg0_ref, wu0_ref, wd0_ref)):
            dst[...] = src[...].astype(BF16)

    def chunk_rows(c):
        return pl.ds(c * ch if isinstance(c, int) else pl.multiple_of(c * ch, ch), ch)

    def step(parity, static_k):
        wg_s, wu_s, wd_s = w_bf16[parity]

        def up(c):
            x = x_ref[chunk_rows(c), :]
            g = jnp.dot(x, wg_s[...], preferred_element_type=F32)
            u = jnp.dot(x, wu_s[...], preferred_element_type=F32)
            return (g * jax.nn.sigmoid(g) * u).astype(BF16)

        def down(c, hmid):
            o_ref[chunk_rows(c), :] += jnp.dot(hmid, wd_s[...], preferred_element_type=F32)

        def fused(c, hmid):
            down(c - 1, hmid)
            return up(c)

        def convert_next(piece, n_pieces):
            for dst, src in zip(w_bf16[1 - parity], (wgn_ref, wun_ref, wdn_ref)):
                rows = src.shape[0] // n_pieces
                sl = slice(piece * rows, (piece + 1) * rows)
                dst[sl, :] = src[sl, :].astype(BF16)

        if static_k is not None and static_k >= 3:
            n_pieces = min(4, static_k - 2)
            pending = [up(0), up(1)]
            for c in range(2, static_k):
                down(c - 2, pending.pop(0))
                pending.append(up(c))
                if c - 2 < n_pieces:
                    convert_next(c - 2, n_pieces)
            down(static_k - 2, pending.pop(0))
            down(static_k - 1, pending.pop(0))
            return

        hmid = up(0)
        if static_k is not None:
            convert_next(0, 1)
            for c in range(1, static_k):
                hmid = fused(c, hmid)
            down(static_k - 1, hmid)
        else:
            convert_next(0, 1)
            pairs = lax.div(n - 1, 2)

            def two(p, hmid):
                return fused(2 * p + 2, fused(2 * p + 1, hmid))

            hmid = lax.fori_loop(0, pairs, two, hmid)
            hmid = lax.fori_loop(2 * pairs + 1, n, fused, hmid)
            down(n - 1, hmid)

    is_static = functools.reduce(jnp.logical_or, [n == k for k in static_n], n < 0)
    for parity in (0, 1):
        mine = j % 2 == parity
        for k in static_n:
            pl.when(jnp.logical_and(mine, n == k))(functools.partial(step, parity, k))
        pl.when(jnp.logical_and(mine, jnp.logical_and(n > 0, jnp.logical_not(is_static))))(
            functools.partial(step, parity, None))

    if ln:
        row0 = gb_ref[s] * cap

        def xres_copy(c, slot):
            return pltpu.make_async_copy(
                xres_hbm.at[pl.ds(row0 + c * ch, ch)], xr_buf.at[slot], xr_sem.at[slot])

        @pl.when(jnp.logical_and(j == nj - 1, n > 0))
        def _():
            xres_copy(0, 0).start()

            def fin(c, carry):
                slot = c % 2

                @pl.when(c + 1 < n)
                def _():
                    xres_copy(c + 1, 1 - slot).start()

                xres_copy(c, slot).wait()
                rows = pl.ds(pl.multiple_of(c * ch, ch), ch)
                o_ref[rows, :] = _resid_ln(xr_buf[slot], o_ref[rows, :],
                                           gate_ref[...], g_ref[...], b_ref[...], alpha)
                return carry

            lax.fori_loop(0, n, fin, 0)


def _ffn(x_rows, w_gate, w_up, w_down, group_expert, group_chunks, group_block, groups_used, *,
         cap, tf, ch, static_n, ln_args=None, seq=None, alpha=None):
    p_rows, d = x_rows.shape
    f = w_gate.shape[-1]
    n_groups = group_expert.shape[0]
    nj = f // tf
    assert nj % 2 == 0
    ln = ln_args is not None

    def following(s, j, ge, gc, gv):
        wrap = j + 1 >= nj
        s2 = jnp.where(wrap, s + 1, s)
        j2 = jnp.where(wrap, 0, j + 1)
        live = s2 < gv[0]
        last = gv[0] - 1
        return ge[jnp.where(live, s2, last)], jnp.where(live, j2, nj - 1)

    def first_up(s, j, ge, gc, gb, gv):
        return ge[0], 0, 0

    def first_down(s, j, ge, gc, gb, gv):
        return ge[0], 0, 0

    def next_up(s, j, ge, gc, gb, gv):
        e, j2 = following(s, j, ge, gc, gv)
        return e, 0, j2

    def next_down(s, j, ge, gc, gb, gv):
        e, j2 = following(s, j, ge, gc, gv)
        return e, j2, 0

    once = pl.Buffered(1)
    in_specs = [
        pl.BlockSpec((cap, d), lambda s, j, ge, gc, gb, gv: (gb[s], 0), pipeline_mode=once),
        pl.BlockSpec((None, d, tf), first_up, pipeline_mode=once),
        pl.BlockSpec((None, d, tf), first_up, pipeline_mode=once),
        pl.BlockSpec((None, tf, d), first_down, pipeline_mode=once),
        pl.BlockSpec((None, d, tf), next_up),
        pl.BlockSpec((None, d, tf), next_up),
        pl.BlockSpec((None, tf, d), next_down),
    ]
    args = [x_rows, w_gate, w_up, w_down, w_gate, w_up, w_down]
    scratch = 2 * [pltpu.VMEM((d, tf), BF16), pltpu.VMEM((d, tf), BF16), pltpu.VMEM((tf, d), BF16)]
    if ln:
        xres, mods, l, ln_g, ln_b = ln_args
        assert seq % cap == 0
        per_b = seq // cap
        ln_g4 = ln_g.reshape(ln_g.shape[0], 2, 1, d)
        ln_b4 = ln_b.reshape(ln_b.shape[0], 2, 1, d)
        vec = lambda: pl.BlockSpec((None, None, 1, d), lambda s, j, ge, gc, gb, gv: (l, 1, 0, 0))
        in_specs += [
            pl.BlockSpec(memory_space=pl.ANY),
            _mod_spec(mods, l, 1, GATE, lambda s, j, ge, gc, gb, gv: gb[s] // per_b),
            vec(), vec(),
        ]
        args += [xres, mods, ln_g4, ln_b4]
        scratch += [pltpu.VMEM((2, ch, d), F32), pltpu.SemaphoreType.DMA((2,))]
    grid_spec = pltpu.PrefetchScalarGridSpec(
        num_scalar_prefetch=4,
        grid=(n_groups, nj),
        in_specs=in_specs,
        out_specs=pl.BlockSpec((cap, d), lambda s, j, ge, gc, gb, gv: (s, 0), pipeline_mode=once),
        scratch_shapes=scratch,
    )
    return pl.pallas_call(
        functools.partial(_ffn_kernel, ch=ch, alpha=alpha, ln=ln, static_n=static_n),
        grid_spec=grid_spec,
        out_shape=jax.ShapeDtypeStruct((p_rows, d), F32),
        compiler_params=_params(("arbitrary", "arbitrary")),
        name="ffn_ln" if ln else "ffn_moe",
    )(group_expert, group_chunks, group_block, groups_used, *args)


def _row_copy(src_hbm, src_row, dst_buf, dst_row, sem):
    return pltpu.make_async_copy(src_hbm.at[pl.ds(src_row, 1)], dst_buf.at[pl.ds(dst_row, 1)], sem)


def _next_tile_spec(n_tiles, width):
    return pl.BlockSpec((None, 1, width), lambda i, *_: (jnp.minimum(i + 1, n_tiles - 1), 0, 0),
                        memory_space=pltpu.SMEM)


def _first_tile_spec(width):
    return pl.BlockSpec((None, 1, width), lambda i, *_: (0, 0, 0), memory_space=pltpu.SMEM)


def _gather_kernel(cnt_ref, tok0_ref, tokn_ref, h_hbm, o_ref, buf, sem, *, tg):
    i = pl.program_id(0)
    n = pl.num_programs(0)

    def bursts(tile):
        return lax.div(cnt_ref[tile], ROW_BURST)

    def issue(tok_ref, tile, slot):
        def body(g, carry):
            for k in range(ROW_BURST):
                r = g * ROW_BURST + k
                _row_copy(h_hbm, tok_ref[0, r], buf.at[slot], r, sem.at[slot]).start()
            return carry
        lax.fori_loop(0, bursts(tile), body, 0)

    def drain(tile, slot):
        def body(g, carry):
            for k in range(ROW_BURST):
                _row_copy(h_hbm, 0, buf.at[slot], g * ROW_BURST + k, sem.at[slot]).wait()
            return carry
        lax.fori_loop(0, bursts(tile), body, 0)

    @pl.when(i == 0)
    def _():
        buf[...] = jnp.zeros_like(buf)
        issue(tok0_ref, 0, 0)

    @pl.when(i + 1 < n)
    def _():
        issue(tokn_ref, i + 1, (i + 1) % 2)

    drain(i, i % 2)
    o_ref[...] = buf[i % 2].astype(o_ref.dtype)


def _gather(h, tok_of_row, tile_rows, p_rows):
    n, d = h.shape
    tg = GATHER_TILE
    n_tiles = p_rows // tg
    tok3 = tok_of_row.reshape(n_tiles, 1, tg)
    grid_spec = pltpu.PrefetchScalarGridSpec(
        num_scalar_prefetch=1,
        grid=(n_tiles,),
        in_specs=[_first_tile_spec(tg), _next_tile_spec(n_tiles, tg),
                  pl.BlockSpec(memory_space=pl.ANY)],
        out_specs=pl.BlockSpec((tg, d), lambda i, cnt: (i, 0)),
        scratch_shapes=[pltpu.VMEM((2, tg, d), F32), pltpu.SemaphoreType.DMA((2,))],
    )
    return pl.pallas_call(
        functools.partial(_gather_kernel, tg=tg),
        grid_spec=grid_spec,
        out_shape=jax.ShapeDtypeStruct((p_rows, d), BF16),
        compiler_params=_params(("arbitrary",)),
        name="gather",
    )(tile_rows, tok3, tok3, h)


def _combine_kernel(dst0_ref, dstn_ref, x_ref, p_ref, gate_ref, g_ref, b_ref, ys_hbm, o_ref,
                    buf, sem, *, tc, alpha):
    i = pl.program_id(0)
    n = pl.num_programs(0)

    def issue(dst_ref, slot):
        def body(r, carry):
            for k in range(TOP_K):
                _row_copy(ys_hbm, dst_ref[0, r * TOP_K + k], buf.at[slot, k], r,
                          sem.at[slot]).start()
            return carry
        lax.fori_loop(0, tc, body, 0, unroll=4)

    def drain(slot):
        def body(r, carry):
            for k in range(TOP_K):
                _row_copy(ys_hbm, 0, buf.at[slot, k], r, sem.at[slot]).wait()
            return carry
        lax.fori_loop(0, tc, body, 0, unroll=4)

    @pl.when(i == 0)
    def _():
        issue(dst0_ref, 0)

    @pl.when(i + 1 < n)
    def _():
        issue(dstn_ref, (i + 1) % 2)

    slot = i % 2
    drain(slot)
    p = p_ref[...]
    o = p[:, 0:1] * buf[slot, 0] + p[:, 1:2] * buf[slot, 1]
    o_ref[...] = _resid_ln(x_ref[...], o, gate_ref[...], g_ref[...], b_ref[...], alpha)


def _combine(ys, dest, x2, probs, mods, l, ln_g, ln_b, seq, alpha):
    n, d = x2.shape
    tc = 256
    n_tiles = n // tc
    per_b = seq // tc
    dest3 = dest.reshape(n_tiles, 1, tc * TOP_K)
    ln_g4 = ln_g.reshape(ln_g.shape[0], 2, 1, d)
    ln_b4 = ln_b.reshape(ln_b.shape[0], 2, 1, d)
    vec = lambda: pl.BlockSpec((None, None, 1, d), lambda i: (l, 1, 0, 0))
    return pl.pallas_call(
        functools.partial(_combine_kernel, tc=tc, alpha=alpha),
        grid=(n_tiles,),
        in_specs=[
            _first_tile_spec(tc * TOP_K), _next_tile_spec(n_tiles, tc * TOP_K),
            pl.BlockSpec((tc, d), lambda i: (i, 0)),
            pl.BlockSpec((tc, LANES), lambda i: (i, 0)),
            _mod_spec(mods, l, 1, GATE, lambda i: i // per_b),
            vec(), vec(),
            pl.BlockSpec(memory_space=pl.ANY),
        ],
        out_specs=pl.BlockSpec((tc, d), lambda i: (i, 0)),
        out_shape=jax.ShapeDtypeStruct((n, d), F32),
        scratch_shapes=[pltpu.VMEM((2, TOP_K, tc, d), F32), pltpu.SemaphoreType.DMA((2,))],
        compiler_params=_params(("arbitrary",)),
        name="combine",
    )(dest3, dest3, x2, probs, mods, ln_g4, ln_b4, ys)


def _routing_tables(top_i, cap, ch, max_groups):
    n = top_i.shape[0]
    e_flat = top_i.reshape(-1)
    onehot = (e_flat[:, None] == jnp.arange(N_EXPERTS, dtype=jnp.int32)[None, :]).astype(jnp.int32)
    csum = jnp.cumsum(onehot, axis=0)
    counts = csum[-1]
    rank = jnp.take_along_axis(csum, e_flat[:, None], axis=1)[:, 0] - 1
    groups_per = (counts + cap - 1) // cap
    group_end = jnp.cumsum(groups_per)
    group_start = group_end - groups_per
    dest = (group_start * cap)[e_flat] + rank
    n_groups = group_end[-1]
    gidx = jnp.arange(max_groups, dtype=jnp.int32)
    gclamp = jnp.minimum(gidx, n_groups - 1)
    group_expert = jnp.sum((group_end[None, :] <= gclamp[:, None]).astype(jnp.int32), axis=1)
    rows = jnp.clip(counts[group_expert] - (gclamp - group_start[group_expert]) * cap, 0, cap)
    group_chunks = jnp.where(gidx < n_groups, (rows + ch - 1) // ch, 0)
    p_rows = max_groups * cap
    tok_of_row = (jnp.arange(p_rows, dtype=jnp.int32) % n).at[dest].set(
        jnp.arange(n * TOP_K, dtype=jnp.int32) // TOP_K)
    tiles_per_group = cap // GATHER_TILE
    in_group = jnp.where(gidx < n_groups, rows, 0)[:, None] - (
        jnp.arange(tiles_per_group, dtype=jnp.int32) * GATHER_TILE)[None, :]
    tile_rows = jnp.clip(in_group, 0, GATHER_TILE).reshape(-1)
    tile_rows = (tile_rows + ROW_BURST - 1) // ROW_BURST * ROW_BURST
    return (dest.astype(jnp.int32), tok_of_row, tile_rows.astype(jnp.int32),
            group_expert.astype(jnp.int32), group_chunks.astype(jnp.int32), gclamp.astype(jnp.int32),
            n_groups.reshape(1).astype(jnp.int32))


def kernel(x, c, ada_w, ada_b, ln_g, ln_b, mix_w_in, conv_w, conv_b, lru_wa, lru_ba, lru_wx, lru_bx,
           lru_lam, pool_w, pool_b, pool_scale, mix_w_out, ffn_w_gate, ffn_w_up, ffn_w_down,
           router_w, exp_w_gate, exp_w_up, exp_w_down):
    nb, seq, d = x.shape
    depth = ada_w.shape[0]
    n = nb * seq
    alpha = float((2 * depth) ** 0.25)
    mods = _ada(c, ada_w, ada_b)
    x2 = x.reshape(n, d)

    row3 = lambda v: v.reshape(depth, 1, -1)
    seq_params = [conv_w, row3(conv_b), lru_wa.astype(BF16), row3(lru_ba), lru_wx.astype(BF16),
                  row3(lru_bx), row3(lru_lam), pool_w.astype(BF16), row3(pool_b), row3(pool_scale)]

    w_in_bf16 = _to_bf16(mix_w_in)

    ffn_tf, ffn_ch = 256, 256
    dense_cap = seq
    moe_cap = 2304
    for l in range(depth):
        moe = (l % 2 == 1)
        i = l // 2

        y = _mixer(x2, mods, l, w_in_bf16, seq_params, nb, seq)
        outs = _mix_out(y, x2, mods, l, ln_g, ln_b, mix_w_out, router_w[i] if moe else None,
                        seq, alpha)
        if not moe:
            x2, h = outs
            n_groups = n // dense_cap
            ge = jnp.full((n_groups,), i, jnp.int32)
            gc = jnp.full((n_groups,), dense_cap // ffn_ch, jnp.int32)
            gb = jnp.arange(n_groups, dtype=jnp.int32)
            gv = jnp.full((1,), n_groups, jnp.int32)
            x2 = _ffn(h, ffn_w_gate, ffn_w_up, ffn_w_down, ge, gc, gb, gv,
                      cap=dense_cap, tf=ffn_tf, ch=ffn_ch, static_n=(dense_cap // ffn_ch,),
                      ln_args=(x2, mods, l, ln_g, ln_b), seq=seq, alpha=alpha)
        else:
            x_mix, h, idx, probs = outs
            top_i = idx[:, :TOP_K]
            n_exp = exp_w_gate.shape[1]
            f_exp = exp_w_gate.shape[-1]

            def moe(max_groups):
                dest, tok_of_row, tile_rows, ge, gc, gb, gv = _routing_tables(
                    top_i, moe_cap, ffn_ch, max_groups)
                xs = _gather(h, tok_of_row, tile_rows, max_groups * moe_cap)
                ys = _ffn(xs,
                          exp_w_gate.reshape(-1, d, f_exp), exp_w_up.reshape(-1, d, f_exp),
                          exp_w_down.reshape(-1, f_exp, d), ge + i * n_exp, gc, gb, gv,
                          cap=moe_cap, tf=ffn_tf, ch=ffn_ch,
                          static_n=(moe_cap // ffn_ch - 1, moe_cap // ffn_ch))
                return _combine(ys, dest, x_mix, probs, mods, l, ln_g, ln_b, seq, alpha)

            x2 = moe((n * TOP_K) // moe_cap + n_exp)
    return x2.reshape(nb, seq, d)
```

```python
import functools

import jax
import jax.numpy as jnp
from jax import lax
from jax.experimental import pallas as pl
from jax.experimental.pallas import tpu as pltpu

F32 = jnp.float32
BF16 = jnp.bfloat16

LN_EPS = 1e-5
LRU_C = 8.0
CONV_WIDTH = 4
LRU_HEADS = 4
POOL_WINDOWS = (2, 4, 8, 16)
N_EXPERTS = 8
TOP_K = 2

V7X_VMEM_LIMIT_BYTES = 58 * 1024 * 1024
LANES = 128
SUBLANES = 8
POOL_HIST = 16
SHIFT, SCALE, GATE = 0, 1, 2
GATHER_TILE = 256
ROW_BURST = 8


def _params(sem, vmem=V7X_VMEM_LIMIT_BYTES):
    return pltpu.CompilerParams(dimension_semantics=sem, vmem_limit_bytes=vmem)


def _mod_spec(mods, l, k, which, batch_of):
    d = mods.shape[-1]
    return pl.BlockSpec((None, None, None, None, 1, d),
                        lambda *g: (l, k, batch_of(*g), which, 0, 0))


def _layer_spec(arr, l):
    nd = arr.ndim - 1
    return pl.BlockSpec((None,) + arr.shape[1:], lambda *g: (l,) + (0,) * nd)


def _ada_kernel(c_ref, w_ref, b_ref, o_ref):
    c = c_ref[...]
    c_act = (c * jax.nn.sigmoid(c)).astype(BF16)
    o_ref[...] = jnp.dot(c_act, w_ref[...].astype(BF16),
                         preferred_element_type=F32) + b_ref[...]


def _ada(c, ada_w, ada_b):
    depth, _, d, d3 = ada_w.shape
    nb = c.shape[0]
    rows = -(-nb // SUBLANES) * SUBLANES
    c_pad = jnp.pad(c, ((0, rows - nb), (0, 0)))
    w = ada_w.reshape(depth * 2, d, d3)
    b = ada_b.reshape(depth * 2, 1, d3)
    tn = 1024
    out = pl.pallas_call(
        _ada_kernel,
        grid=(depth * 2, d3 // tn),
        in_specs=[
            pl.BlockSpec((rows, d), lambda l, j: (0, 0)),
            pl.BlockSpec((None, d, tn), lambda l, j: (l, 0, j)),
            pl.BlockSpec((None, 1, tn), lambda l, j: (l, 0, j)),
        ],
        out_specs=pl.BlockSpec((None, rows, tn), lambda l, j: (l, 0, j)),
        out_shape=jax.ShapeDtypeStruct((depth * 2, rows, d3), F32),
        compiler_params=_params(("arbitrary", "arbitrary")),
        name="ada",
    )(c_pad, w, b)
    return out.reshape(depth, 2, rows, 3, 1, d)


def _to_bf16_kernel(w_ref, o_ref):
    o_ref[...] = w_ref[...].astype(BF16)


def _to_bf16(w):
    shape = w.shape
    w2 = w.reshape(-1, shape[-1])
    tr = 512
    out = pl.pallas_call(
        _to_bf16_kernel,
        grid=(w2.shape[0] // tr,),
        in_specs=[pl.BlockSpec((tr, shape[-1]), lambda i: (i, 0))],
        out_specs=pl.BlockSpec((tr, shape[-1]), lambda i: (i, 0)),
        out_shape=jax.ShapeDtypeStruct(w2.shape, BF16),
        compiler_params=_params(("arbitrary",)),
        name="to_bf16",
    )(w2)
    return out.reshape(shape)


def _neg_expm1_nonpos(v, exp_v):
    poly = 1.0 / 120.0
    for coef in (1.0 / 24.0, 1.0 / 6.0, 0.5, 1.0):
        poly = poly * v + coef
    return jnp.where(v > -0.01, -v * poly, 1.0 - exp_v)


def _lru_scan(a, u, h_carry, cols, ts):
    row = lax.broadcasted_iota(jnp.int32, (ts, 1), 0) % SUBLANES
    d = 1
    while d < SUBLANES:
        keep = row >= d
        a_prev = jnp.where(keep, pltpu.roll(a, d, 0), 1.0)
        u_prev = jnp.where(keep, pltpu.roll(u, d, 0), 0.0)
        u = u + a * u_prev
        a = a * a_prev
        d *= 2
    carry = h_carry[0:1, cols]
    groups = []
    for g in range(ts // SUBLANES):
        rows = slice(g * SUBLANES, (g + 1) * SUBLANES)
        blk = a[rows] * carry + u[rows]
        groups.append(blk)
        carry = blk[SUBLANES - 1:SUBLANES]
    h_carry[:, cols] = jnp.broadcast_to(carry, (h_carry.shape[0], carry.shape[1]))
    return jnp.concatenate(groups, axis=0)


def _lru_head(z_ref, small, y_ref, conv_buf, h_carry, hd, ts):
    cw_ref, cb_ref, wa_ref, ba_ref, wx_ref, bx_ref, lam_ref = small[:7]
    d_lru = cw_ref.shape[-1]
    hdim = wa_ref.shape[-1]
    cols = slice(hd * hdim, (hd + 1) * hdim)

    xl = z_ref[:, cols]
    conv_buf[SUBLANES:, cols] = xl
    cw = cw_ref[:, cols]
    xc = cb_ref[:, cols] + cw[CONV_WIDTH - 1:CONV_WIDTH] * xl
    for k in range(CONV_WIDTH - 1):
        back = CONV_WIDTH - 1 - k
        xc = xc + cw[k:k + 1] * conv_buf[SUBLANES - back:SUBLANES - back + ts, cols]
    conv_buf[:SUBLANES, cols] = xl[ts - SUBLANES:]

    xcb = xc.astype(BF16)
    gate_a = jax.nn.sigmoid(
        jnp.dot(xcb, wa_ref[hd], preferred_element_type=F32) + ba_ref[:, cols])
    gate_x = jax.nn.sigmoid(
        jnp.dot(xcb, wx_ref[hd], preferred_element_type=F32) + bx_ref[:, cols])
    log_a = (-LRU_C) * gate_a * jax.nn.softplus(-lam_ref[:, cols])
    a = jnp.exp(log_a)
    m2 = _neg_expm1_nonpos(2.0 * log_a, a * a)
    mult = jnp.where(m2 > 0.0, m2 * lax.rsqrt(m2), 0.0)
    u = (xc * gate_x) * mult
    hs = _lru_scan(a, u, h_carry, cols, ts)
    gl = z_ref[:, d_lru + hd * hdim:d_lru + (hd + 1) * hdim]
    y_ref[:, cols] = (hs * jax.nn.gelu(gl)).astype(y_ref.dtype)


def _pool_group(z_ref, small, y_ref, pool_hist, gi, t, ts):
    d_lru = small[0].shape[-1]
    pw_ref, pb_ref, ps_ref = small[7:]
    gd = pw_ref.shape[-1]
    w = POOL_WINDOWS[gi]
    cols = slice(gi * gd, (gi + 1) * gd)
    row = lax.broadcasted_iota(jnp.int32, (ts, 1), 0)
    xp = z_ref[:, 2 * d_lru + gi * gd:2 * d_lru + (gi + 1) * gd]
    s = jnp.concatenate([pool_hist[:, cols], xp], axis=0)
    pool_hist[:, cols] = xp[ts - POOL_HIST:]
    sh = 1
    while sh < w:
        s = s + pltpu.roll(s, sh, 0)
        sh *= 2
    cnt = jnp.minimum(t * ts + row + 1, w).astype(F32)
    p = (s[POOL_HIST:] / cnt - xp).astype(BF16)
    yp = (jnp.dot(p, pw_ref[gi], preferred_element_type=F32) + pb_ref[:, cols]) * ps_ref[:, cols]
    y_ref[:, d_lru + gi * gd:d_lru + (gi + 1) * gd] = yp.astype(y_ref.dtype)


def _mixer_kernel(x_ref, shift_ref, scale_ref, w_ref, *rest, ts, nt):
    small = rest[:10]
    y_ref, z_even, z_odd, h_mod, conv_buf, pool_hist, h_carry = rest[10:]
    t = pl.program_id(1)
    n_parts = LRU_HEADS
    assert len(POOL_WINDOWS) == n_parts
    slab = w_ref.shape[1] // n_parts

    def modulate():
        h_mod[...] = (x_ref[...] * (1.0 + scale_ref[...]) + shift_ref[...]).astype(BF16)

    def part(i, z_dst, z_src):
        if z_dst is not None:
            cols = slice(i * slab, (i + 1) * slab)
            z_dst[:, cols] = jnp.dot(h_mod[...], w_ref[:, cols], preferred_element_type=F32)
        if z_src is not None:
            _lru_head(z_src, small, y_ref, conv_buf, h_carry, i, ts)
            _pool_group(z_src, small, y_ref, pool_hist, i, t - 1, ts)

    def parts(z_dst, z_src):
        for i in range(n_parts):
            part(i, z_dst, z_src)

    @pl.when(t == 0)
    def _():
        conv_buf[...] = jnp.zeros_like(conv_buf)
        pool_hist[...] = jnp.zeros_like(pool_hist)
        h_carry[...] = jnp.zeros_like(h_carry)
        modulate()
        parts(z_even, None)

    inner = jnp.logical_and(t > 0, t < nt)

    @pl.when(jnp.logical_and(inner, t % 2 == 1))
    def _():
        modulate()
        parts(z_odd, z_even)

    @pl.when(jnp.logical_and(inner, t % 2 == 0))
    def _():
        modulate()
        parts(z_even, z_odd)

    @pl.when(t == nt)
    def _():
        parts(None, z_odd if (nt - 1) % 2 == 1 else z_even)


def _mixer(x2, mods, l, w_in_bf16, small, nb, seq):
    n, d = x2.shape
    d_in = w_in_bf16.shape[-1]
    d_lru = small[0].shape[-1]
    assert d_in == 3 * d_lru
    ts = 256
    nt = seq // ts
    batch_of = lambda b, t: b
    return pl.pallas_call(
        functools.partial(_mixer_kernel, ts=ts, nt=nt),
        grid=(nb, nt + 1),
        in_specs=[
            pl.BlockSpec((ts, d), lambda b, t: (b * nt + jnp.minimum(t, nt - 1), 0)),
            _mod_spec(mods, l, 0, SHIFT, batch_of),
            _mod_spec(mods, l, 0, SCALE, batch_of),
            pl.BlockSpec((None, d, d_in), lambda b, t: (l, 0, 0), pipeline_mode=pl.Buffered(1)),
        ] + [_layer_spec(a, l) for a in small],
        out_specs=pl.BlockSpec((ts, 2 * d_lru), lambda b, t: (b * nt + jnp.maximum(t - 1, 0), 0)),
        out_shape=jax.ShapeDtypeStruct((n, 2 * d_lru), BF16),
        scratch_shapes=[pltpu.VMEM((ts, d_in), F32), pltpu.VMEM((ts, d_in), F32),
                        pltpu.VMEM((ts, d), BF16),
                        pltpu.VMEM((SUBLANES + ts, d_lru), F32),
                        pltpu.VMEM((POOL_HIST, d_lru), F32),
                        pltpu.VMEM((SUBLANES, d_lru), F32)],
        compiler_params=_params(("arbitrary", "arbitrary")),
        name="mixer",
    )(x2, mods, mods, w_in_bf16, *small)


def _resid_ln(x, o, gate, g, b, alpha):
    v = alpha * x + (1.0 + gate) * o
    mu = jnp.mean(v, axis=-1, keepdims=True)
    dv = v - mu
    var = jnp.mean(dv * dv, axis=-1, keepdims=True)
    return dv * lax.rsqrt(var + LN_EPS) * g + b


def _route(h, wr_ref):
    tm = h.shape[0]
    h_hi = h.astype(BF16)
    h_lo = (h - h_hi.astype(F32)).astype(BF16)
    r = jnp.dot(jnp.concatenate([h_hi, h_lo], axis=0), wr_ref[...], preferred_element_type=F32)
    r = r[:tm] + r[tm:]
    logits = r + pltpu.roll(r, LANES - N_EXPERTS, 1)
    lane = lax.broadcasted_iota(jnp.int32, logits.shape, 1).astype(F32)
    neg = jnp.float32(-jnp.inf)
    lg = jnp.where(lane < N_EXPERTS, logits, neg)
    m1 = jnp.max(lg, axis=-1, keepdims=True)
    i1 = jnp.min(jnp.where(lg == m1, lane, float(LANES)), axis=-1, keepdims=True)
    lg2 = jnp.where(lane == i1, neg, lg)
    m2 = jnp.max(lg2, axis=-1, keepdims=True)
    i2 = jnp.min(jnp.where(lg2 == m2, lane, float(LANES)), axis=-1, keepdims=True)
    e = jnp.exp(m2 - m1)
    p1 = 1.0 / (1.0 + e)
    p2 = e / (1.0 + e)
    idx = jnp.where(lane == 0, i1, i2).astype(jnp.int32)
    return idx, jnp.where(lane == 0, p1, p2)


def _router_weights(router_w):
    hi = router_w.astype(BF16)
    lo = (router_w - hi.astype(F32)).astype(BF16)
    wr = jnp.concatenate([hi, lo], axis=1)
    return jnp.pad(wr, ((0, 0), (0, LANES - wr.shape[1])))


def _mix_out_kernel(y_ref, x_ref, gate_ref, g_ref, b_ref, shift_ref, scale_ref, w_ref, *rest,
                    alpha, route):
    if route:
        wr_ref, xo_ref, h_ref, idx_ref, p_ref, w_scr = rest
    else:
        xo_ref, h_ref, w_scr = rest

    @pl.when(pl.program_id(0) == 0)
    def _():
        w_scr[...] = w_ref[...].astype(BF16)

    o = jnp.dot(y_ref[...], w_scr[...], preferred_element_type=F32)
    xn = _resid_ln(x_ref[...], o, gate_ref[...], g_ref[...], b_ref[...], alpha)
    xo_ref[...] = xn
    h = xn * (1.0 + scale_ref[...]) + shift_ref[...]
    h_ref[...] = h.astype(h_ref.dtype)
    if route:
        idx, p = _route(h, wr_ref)
        idx_ref[...] = idx
        p_ref[...] = p


def _mix_out(y, x2, mods, l, ln_g, ln_b, w_out_all, router_w, seq, alpha):
    n, d = x2.shape
    route = router_w is not None
    tm = 256
    per_b = seq // tm
    batch_of = lambda i: i // per_b
    tok = lambda: pl.BlockSpec((tm, d), lambda i: (i, 0))
    vec = lambda k: pl.BlockSpec((None, None, 1, d), lambda i: (l, k, 0, 0))
    ln_g4 = ln_g.reshape(ln_g.shape[0], 2, 1, d)
    ln_b4 = ln_b.reshape(ln_b.shape[0], 2, 1, d)
    in_specs = [tok(), tok(), _mod_spec(mods, l, 0, GATE, batch_of), vec(0), vec(0),
                _mod_spec(mods, l, 1, SHIFT, batch_of), _mod_spec(mods, l, 1, SCALE, batch_of),
                pl.BlockSpec((None, d, d), lambda i: (l, 0, 0), pipeline_mode=pl.Buffered(1))]
    args = [y, x2, mods, ln_g4, ln_b4, mods, mods, w_out_all]
    out_specs = [tok(), tok()]
    out_shape = [jax.ShapeDtypeStruct((n, d), F32),
                 jax.ShapeDtypeStruct((n, d), F32 if route else BF16)]
    if route:
        in_specs.append(pl.BlockSpec((d, LANES), lambda i: (0, 0)))
        args.append(_router_weights(router_w))
        out_specs += [pl.BlockSpec((tm, LANES), lambda i: (i, 0))] * 2
        out_shape += [jax.ShapeDtypeStruct((n, LANES), jnp.int32),
                      jax.ShapeDtypeStruct((n, LANES), F32)]
    return pl.pallas_call(
        functools.partial(_mix_out_kernel, alpha=alpha, route=route),
        grid=(n // tm,),
        in_specs=in_specs,
        out_specs=out_specs,
        out_shape=out_shape,
        scratch_shapes=[pltpu.VMEM((d, d), BF16)],
        compiler_params=_params(("arbitrary",)),
        name="mix_out",
    )(*args)


def _ffn_kernel(ge_ref, gc_ref, gb_ref, gv_ref, gm_ref, x_ref, wg0_ref, wu0_ref, wd0_ref, wgn_ref,
                wun_ref, wdn_ref, *rest, ch, alpha, ln, static_modes):
    if ln:
        xres_hbm, gate_ref, g_ref, b_ref, o_ref = rest[:5]
        bufs, (xr_buf, xr_sem) = rest[5:11], rest[11:]
    else:
        o_ref, bufs = rest[0], rest[1:7]
    w_bf16 = (bufs[:3], bufs[3:])
    s = pl.program_id(0)
    j = pl.program_id(1)
    nj = pl.num_programs(1)
    n = gc_ref[s]
    cap = x_ref.shape[0]

    @pl.when(j == 0)
    def _():
        o_ref[...] = jnp.zeros_like(o_ref)

    @pl.when(jnp.logical_and(s == 0, j == 0))
    def _():
        for dst, src in zip(w_bf16[0], (wg0_ref, wu0_ref, wd0_ref)):
            dst[...] = src[...].astype(BF16)

    def step(parity, mode):
        wg_s, wu_s, wd_s = w_bf16[parity]
        static_k, rows_per_chunk = (None, ch) if mode is None else mode

        def chunk_rows(c):
            start = c * rows_per_chunk
            return pl.ds(start if isinstance(c, int) else pl.multiple_of(start, ch),
                         rows_per_chunk)

        def up(c):
            x = x_ref[chunk_rows(c), :]
            g = jnp.dot(x, wg_s[...], preferred_element_type=F32)
            u = jnp.dot(x, wu_s[...], preferred_element_type=F32)
            return (g * jax.nn.sigmoid(g) * u).astype(BF16)

        def down(c, hmid):
            o_ref[chunk_rows(c), :] += jnp.dot(hmid, wd_s[...], preferred_element_type=F32)

        def fused(c, hmid):
            down(c - 1, hmid)
            return up(c)

        def convert_next(piece, n_pieces):
            for dst, src in zip(w_bf16[1 - parity], (wgn_ref, wun_ref, wdn_ref)):
                rows = src.shape[0] // n_pieces
                sl = slice(piece * rows, (piece + 1) * rows)
                dst[sl, :] = src[sl, :].astype(BF16)

        if static_k is not None and static_k >= 3:
            n_pieces = min(4, static_k - 2)
            pending = [up(0), up(1)]
            for c in range(2, static_k):
                down(c - 2, pending.pop(0))
                pending.append(up(c))
                if c - 2 < n_pieces:
                    convert_next(c - 2, n_pieces)
            down(static_k - 2, pending.pop(0))
            down(static_k - 1, pending.pop(0))
            return

        hmid = up(0)
        if static_k is not None:
            convert_next(0, 1)
            for c in range(1, static_k):
                hmid = fused(c, hmid)
            down(static_k - 1, hmid)
        else:
            convert_next(0, 1)
            pairs = lax.div(n - 1, 2)

            def two(p, hmid):
                return fused(2 * p + 2, fused(2 * p + 1, hmid))

            hmid = lax.fori_loop(0, pairs, two, hmid)
            hmid = lax.fori_loop(2 * pairs + 1, n, fused, hmid)
            down(n - 1, hmid)

    form = gm_ref[s]
    for parity in (0, 1):
        mine = jnp.logical_and(j % 2 == parity, n > 0)
        for i, mode in enumerate(static_modes):
            pl.when(jnp.logical_and(mine, form == i + 1))(functools.partial(step, parity, mode))
        pl.when(jnp.logical_and(mine, form == 0))(functools.partial(step, parity, None))

    if ln:
        row0 = gb_ref[s] * cap

        def xres_copy(c, slot):
            return pltpu.make_async_copy(
                xres_hbm.at[pl.ds(row0 + c * ch, ch)], xr_buf.at[slot], xr_sem.at[slot])

        @pl.when(jnp.logical_and(j == nj - 1, n > 0))
        def _():
            xres_copy(0, 0).start()

            def fin(c, carry):
                slot = c % 2

                @pl.when(c + 1 < n)
                def _():
                    xres_copy(c + 1, 1 - slot).start()

                xres_copy(c, slot).wait()
                rows = pl.ds(pl.multiple_of(c * ch, ch), ch)
                o_ref[rows, :] = _resid_ln(xr_buf[slot], o_ref[rows, :],
                                           gate_ref[...], g_ref[...], b_ref[...], alpha)
                return carry

            lax.fori_loop(0, n, fin, 0)


def _ffn(x_rows, w_gate, w_up, w_down, group_expert, group_chunks, group_block, groups_used,
         group_form, *, cap, tf, ch, static_modes, ln_args=None, seq=None, alpha=None):
    p_rows, d = x_rows.shape
    f = w_gate.shape[-1]
    n_groups = group_expert.shape[0]
    nj = f // tf
    assert nj % 2 == 0
    ln = ln_args is not None

    def following(s, j, ge, gv):
        wrap = j + 1 >= nj
        s2 = jnp.where(wrap, s + 1, s)
        j2 = jnp.where(wrap, 0, j + 1)
        live = s2 < gv[0]
        last = gv[0] - 1
        return ge[jnp.where(live, s2, last)], jnp.where(live, j2, nj - 1)

    def first(s, j, ge, gc, gb, gv, gm):
        return ge[0], 0, 0

    def next_up(s, j, ge, gc, gb, gv, gm):
        e, j2 = following(s, j, ge, gv)
        return e, 0, j2

    def next_down(s, j, ge, gc, gb, gv, gm):
        e, j2 = following(s, j, ge, gv)
        return e, j2, 0

    once = pl.Buffered(1)
    in_specs = [
        pl.BlockSpec((cap, d), lambda s, j, ge, gc, gb, gv, gm: (gb[s], 0), pipeline_mode=once),
        pl.BlockSpec((None, d, tf), first, pipeline_mode=once),
        pl.BlockSpec((None, d, tf), first, pipeline_mode=once),
        pl.BlockSpec((None, tf, d), first, pipeline_mode=once),
        pl.BlockSpec((None, d, tf), next_up),
        pl.BlockSpec((None, d, tf), next_up),
        pl.BlockSpec((None, tf, d), next_down),
    ]
    args = [x_rows, w_gate, w_up, w_down, w_gate, w_up, w_down]
    scratch = 2 * [pltpu.VMEM((d, tf), BF16), pltpu.VMEM((d, tf), BF16), pltpu.VMEM((tf, d), BF16)]
    if ln:
        xres, mods, l, ln_g, ln_b = ln_args
        assert seq % cap == 0
        per_b = seq // cap
        ln_g4 = ln_g.reshape(ln_g.shape[0], 2, 1, d)
        ln_b4 = ln_b.reshape(ln_b.shape[0], 2, 1, d)
        vec = lambda: pl.BlockSpec((None, None, 1, d), lambda s, j, *tables: (l, 1, 0, 0))
        in_specs += [
            pl.BlockSpec(memory_space=pl.ANY),
            _mod_spec(mods, l, 1, GATE, lambda s, j, ge, gc, gb, gv, gm: gb[s] // per_b),
            vec(), vec(),
        ]
        args += [xres, mods, ln_g4, ln_b4]
        scratch += [pltpu.VMEM((2, ch, d), F32), pltpu.SemaphoreType.DMA((2,))]
    grid_spec = pltpu.PrefetchScalarGridSpec(
        num_scalar_prefetch=5,
        grid=(n_groups, nj),
        in_specs=in_specs,
        out_specs=pl.BlockSpec((cap, d), lambda s, j, *tables: (s, 0), pipeline_mode=once),
        scratch_shapes=scratch,
    )
    return pl.pallas_call(
        functools.partial(_ffn_kernel, ch=ch, alpha=alpha, ln=ln, static_modes=static_modes),
        grid_spec=grid_spec,
        out_shape=jax.ShapeDtypeStruct((p_rows, d), F32),
        compiler_params=_params(("arbitrary", "arbitrary")),
        name="ffn_ln" if ln else "ffn_moe",
    )(group_expert, group_chunks, group_block, groups_used, group_form, *args)


def _row_copy(src_hbm, src_row, dst_buf, dst_row, sem):
    return pltpu.make_async_copy(src_hbm.at[pl.ds(src_row, 1)], dst_buf.at[pl.ds(dst_row, 1)], sem)


def _next_tile_spec(n_tiles, width):
    return pl.BlockSpec((None, 1, width), lambda i, *_: (jnp.minimum(i + 1, n_tiles - 1), 0, 0),
                        memory_space=pltpu.SMEM)


def _first_tile_spec(width):
    return pl.BlockSpec((None, 1, width), lambda i, *_: (0, 0, 0), memory_space=pltpu.SMEM)


def _gather_kernel(cnt_ref, tok0_ref, tokn_ref, h_hbm, o_ref, buf, sem, *, tg):
    i = pl.program_id(0)
    n = pl.num_programs(0)

    def bursts(tile):
        return lax.div(cnt_ref[tile], ROW_BURST)

    def issue(tok_ref, tile, slot):
        def body(g, carry):
            for k in range(ROW_BURST):
                r = g * ROW_BURST + k
                _row_copy(h_hbm, tok_ref[0, r], buf.at[slot], r, sem.at[slot]).start(
                    priority=k % 2)
            return carry
        lax.fori_loop(0, bursts(tile), body, 0)

    def drain(tile, slot):
        def body(g, carry):
            for k in range(ROW_BURST):
                _row_copy(h_hbm, 0, buf.at[slot], g * ROW_BURST + k, sem.at[slot]).wait()
            return carry
        lax.fori_loop(0, bursts(tile), body, 0)

    @pl.when(i == 0)
    def _():
        buf[...] = jnp.zeros_like(buf)
        issue(tok0_ref, 0, 0)

    @pl.when(i + 1 < n)
    def _():
        issue(tokn_ref, i + 1, (i + 1) % 2)

    drain(i, i % 2)
    o_ref[...] = buf[i % 2].astype(o_ref.dtype)


def _gather(h, tok_of_row, tile_rows, p_rows):
    n, d = h.shape
    tg = GATHER_TILE
    n_tiles = p_rows // tg
    tok3 = tok_of_row.reshape(n_tiles, 1, tg)
    grid_spec = pltpu.PrefetchScalarGridSpec(
        num_scalar_prefetch=1,
        grid=(n_tiles,),
        in_specs=[_first_tile_spec(tg), _next_tile_spec(n_tiles, tg),
                  pl.BlockSpec(memory_space=pl.ANY)],
        out_specs=pl.BlockSpec((tg, d), lambda i, cnt: (i, 0)),
        scratch_shapes=[pltpu.VMEM((2, tg, d), F32), pltpu.SemaphoreType.DMA((2,))],
    )
    return pl.pallas_call(
        functools.partial(_gather_kernel, tg=tg),
        grid_spec=grid_spec,
        out_shape=jax.ShapeDtypeStruct((p_rows, d), BF16),
        compiler_params=_params(("arbitrary",)),
        name="gather",
    )(tile_rows, tok3, tok3, h)


def _combine_kernel(dst0_ref, dstn_ref, x_ref, p_ref, gate_ref, g_ref, b_ref, ys_hbm, o_ref,
                    buf, sem, *, tc, alpha):
    i = pl.program_id(0)
    n = pl.num_programs(0)

    def issue(dst_ref, slot):
        def body(r, carry):
            for k in range(TOP_K):
                _row_copy(ys_hbm, dst_ref[0, r * TOP_K + k], buf.at[slot, k], r,
                          sem.at[slot]).start(priority=k % 2)
            return carry
        lax.fori_loop(0, tc, body, 0, unroll=4)

    def drain(slot):
        def body(r, carry):
            for k in range(TOP_K):
                _row_copy(ys_hbm, 0, buf.at[slot, k], r, sem.at[slot]).wait()
            return carry
        lax.fori_loop(0, tc, body, 0, unroll=4)

    @pl.when(i == 0)
    def _():
        issue(dst0_ref, 0)

    @pl.when(i + 1 < n)
    def _():
        issue(dstn_ref, (i + 1) % 2)

    slot = i % 2
    drain(slot)
    p = p_ref[...]
    o = p[:, 0:1] * buf[slot, 0] + p[:, 1:2] * buf[slot, 1]
    o_ref[...] = _resid_ln(x_ref[...], o, gate_ref[...], g_ref[...], b_ref[...], alpha)


def _combine(ys, dest, x2, probs, mods, l, ln_g, ln_b, seq, alpha):
    n, d = x2.shape
    tc = 256
    n_tiles = n // tc
    per_b = seq // tc
    dest3 = dest.reshape(n_tiles, 1, tc * TOP_K)
    ln_g4 = ln_g.reshape(ln_g.shape[0], 2, 1, d)
    ln_b4 = ln_b.reshape(ln_b.shape[0], 2, 1, d)
    vec = lambda: pl.BlockSpec((None, None, 1, d), lambda i: (l, 1, 0, 0))
    return pl.pallas_call(
        functools.partial(_combine_kernel, tc=tc, alpha=alpha),
        grid=(n_tiles,),
        in_specs=[
            _first_tile_spec(tc * TOP_K), _next_tile_spec(n_tiles, tc * TOP_K),
            pl.BlockSpec((tc, d), lambda i: (i, 0)),
            pl.BlockSpec((tc, LANES), lambda i: (i, 0)),
            _mod_spec(mods, l, 1, GATE, lambda i: i // per_b),
            vec(), vec(),
            pl.BlockSpec(memory_space=pl.ANY),
        ],
        out_specs=pl.BlockSpec((tc, d), lambda i: (i, 0)),
        out_shape=jax.ShapeDtypeStruct((n, d), F32),
        scratch_shapes=[pltpu.VMEM((2, TOP_K, tc, d), F32), pltpu.SemaphoreType.DMA((2,))],
        compiler_params=_params(("arbitrary",)),
        name="combine",
    )(dest3, dest3, x2, probs, mods, ln_g4, ln_b4, ys)


def _routing_tables(top_i, cap, ch, static_modes, max_groups):
    n = top_i.shape[0]
    e_flat = top_i.reshape(-1)
    onehot = (e_flat[:, None] == jnp.arange(N_EXPERTS, dtype=jnp.int32)[None, :]).astype(jnp.int32)
    csum = jnp.cumsum(onehot, axis=0)
    counts = csum[-1]
    rank = jnp.take_along_axis(csum, e_flat[:, None], axis=1)[:, 0] - 1
    groups_per = (counts + cap - 1) // cap
    group_end = jnp.cumsum(groups_per)
    group_start = group_end - groups_per
    dest = (group_start * cap)[e_flat] + rank
    n_groups = group_end[-1]
    gidx = jnp.arange(max_groups, dtype=jnp.int32)
    gclamp = jnp.minimum(gidx, n_groups - 1)
    group_expert = jnp.sum((group_end[None, :] <= gclamp[:, None]).astype(jnp.int32), axis=1)
    rows = jnp.clip(counts[group_expert] - (gclamp - group_start[group_expert]) * cap, 0, cap)
    group_chunks = jnp.where(gidx < n_groups, (rows + ch - 1) // ch, 0)
    group_form = jnp.zeros_like(group_chunks)
    for i, (k, rpc) in reversed(list(enumerate(static_modes))):
        fits = jnp.logical_and(rows <= k * rpc, rows > (k - 1) * ch)
        group_form = jnp.where(fits, i + 1, group_form)
    p_rows = max_groups * cap
    tok_of_row = (jnp.arange(p_rows, dtype=jnp.int32) % n).at[dest].set(
        jnp.arange(n * TOP_K, dtype=jnp.int32) // TOP_K)
    tiles_per_group = cap // GATHER_TILE
    in_group = jnp.where(gidx < n_groups, rows, 0)[:, None] - (
        jnp.arange(tiles_per_group, dtype=jnp.int32) * GATHER_TILE)[None, :]
    tile_rows = jnp.clip(in_group, 0, GATHER_TILE).reshape(-1)
    tile_rows = (tile_rows + ROW_BURST - 1) // ROW_BURST * ROW_BURST
    return (dest.astype(jnp.int32), tok_of_row, tile_rows.astype(jnp.int32),
            group_expert.astype(jnp.int32), group_chunks.astype(jnp.int32), gclamp.astype(jnp.int32),
            n_groups.reshape(1).astype(jnp.int32), group_form.astype(jnp.int32))


def kernel(x, c, ada_w, ada_b, ln_g, ln_b, mix_w_in, conv_w, conv_b, lru_wa, lru_ba, lru_wx, lru_bx,
           lru_lam, pool_w, pool_b, pool_scale, mix_w_out, ffn_w_gate, ffn_w_up, ffn_w_down,
           router_w, exp_w_gate, exp_w_up, exp_w_down):
    nb, seq, d = x.shape
    depth = ada_w.shape[0]
    n = nb * seq
    alpha = float((2 * depth) ** 0.25)
    mods = _ada(c, ada_w, ada_b)
    x2 = x.reshape(n, d)

    row3 = lambda v: v.reshape(depth, 1, -1)
    seq_params = [conv_w, row3(conv_b), lru_wa.astype(BF16), row3(lru_ba), lru_wx.astype(BF16),
                  row3(lru_bx), row3(lru_lam), pool_w.astype(BF16), row3(pool_b), row3(pool_scale)]

    w_in_bf16 = _to_bf16(mix_w_in)

    ffn_tf, ffn_ch = 256, 256
    dense_cap = seq
    moe_cap = 2304
    moe_modes = tuple((8, rpc) for rpc in (256, 272, 288))
    for l in range(depth):
        moe = (l % 2 == 1)
        i = l // 2

        y = _mixer(x2, mods, l, w_in_bf16, seq_params, nb, seq)
        outs = _mix_out(y, x2, mods, l, ln_g, ln_b, mix_w_out, router_w[i] if moe else None,
                        seq, alpha)
        if not moe:
            x2, h = outs
            n_groups = n // dense_cap
            ge = jnp.full((n_groups,), i, jnp.int32)
            gc = jnp.full((n_groups,), dense_cap // ffn_ch, jnp.int32)
            gb = jnp.arange(n_groups, dtype=jnp.int32)
            gv = jnp.full((1,), n_groups, jnp.int32)
            gm = jnp.ones((n_groups,), jnp.int32)
            x2 = _ffn(h, ffn_w_gate, ffn_w_up, ffn_w_down, ge, gc, gb, gv, gm,
                      cap=dense_cap, tf=ffn_tf, ch=ffn_ch,
                      static_modes=((dense_cap // ffn_ch, ffn_ch),),
                      ln_args=(x2, mods, l, ln_g, ln_b), seq=seq, alpha=alpha)
        else:
            x_mix, h, idx, probs = outs
            top_i = idx[:, :TOP_K]
            n_exp = exp_w_gate.shape[1]
            f_exp = exp_w_gate.shape[-1]

            def moe(max_groups):
                dest, tok_of_row, tile_rows, ge, gc, gb, gv, gm = _routing_tables(
                    top_i, moe_cap, ffn_ch, moe_modes, max_groups)
                xs = _gather(h, tok_of_row, tile_rows, max_groups * moe_cap)
                ys = _ffn(xs,
                          exp_w_gate.reshape(-1, d, f_exp), exp_w_up.reshape(-1, d, f_exp),
                          exp_w_down.reshape(-1, f_exp, d), ge + i * n_exp, gc, gb, gv, gm,
                          cap=moe_cap, tf=ffn_tf, ch=ffn_ch, static_modes=moe_modes)
                return _combine(ys, dest, x_mix, probs, mods, l, ln_g, ln_b, seq, alpha)

            x2 = moe((n * TOP_K) // moe_cap + n_exp)
    return x2.reshape(nb, seq, d)
```

```python
import functools

import jax
import jax.numpy as jnp
from jax import lax
from jax.experimental import pallas as pl
from jax.experimental.pallas import tpu as pltpu

F32 = jnp.float32
BF16 = jnp.bfloat16

LN_EPS = 1e-5
LRU_C = 8.0
CONV_WIDTH = 4
LRU_HEADS = 4
POOL_WINDOWS = (2, 4, 8, 16)
N_EXPERTS = 8
TOP_K = 2

V7X_VMEM_LIMIT_BYTES = 58 * 1024 * 1024
LANES = 128
SUBLANES = 8
POOL_HIST = 16
SHIFT, SCALE, GATE = 0, 1, 2
GATHER_TILE = 256
ROW_BURST = 8


def _params(sem, vmem=V7X_VMEM_LIMIT_BYTES):
    return pltpu.CompilerParams(dimension_semantics=sem, vmem_limit_bytes=vmem)


def _mod_spec(mods, l, k, which, batch_of):
    d = mods.shape[-1]
    return pl.BlockSpec((None, None, None, None, 1, d),
                        lambda *g: (l, k, batch_of(*g), which, 0, 0))


def _layer_spec(arr, l):
    nd = arr.ndim - 1
    return pl.BlockSpec((None,) + arr.shape[1:], lambda *g: (l,) + (0,) * nd)


def _ada_kernel(c_ref, w_ref, b_ref, o_ref):
    c = c_ref[...]
    c_act = (c * jax.nn.sigmoid(c)).astype(BF16)
    o_ref[...] = jnp.dot(c_act, w_ref[...].astype(BF16),
                         preferred_element_type=F32) + b_ref[...]


def _ada(c, ada_w, ada_b):
    depth, _, d, d3 = ada_w.shape
    nb = c.shape[0]
    rows = -(-nb // SUBLANES) * SUBLANES
    c_pad = jnp.pad(c, ((0, rows - nb), (0, 0)))
    w = ada_w.reshape(depth * 2, d, d3)
    b = ada_b.reshape(depth * 2, 1, d3)
    tn = 1024
    out = pl.pallas_call(
        _ada_kernel,
        grid=(depth * 2, d3 // tn),
        in_specs=[
            pl.BlockSpec((rows, d), lambda l, j: (0, 0)),
            pl.BlockSpec((None, d, tn), lambda l, j: (l, 0, j)),
            pl.BlockSpec((None, 1, tn), lambda l, j: (l, 0, j)),
        ],
        out_specs=pl.BlockSpec((None, rows, tn), lambda l, j: (l, 0, j)),
        out_shape=jax.ShapeDtypeStruct((depth * 2, rows, d3), F32),
        compiler_params=_params(("arbitrary", "arbitrary")),
        name="ada",
    )(c_pad, w, b)
    return out.reshape(depth, 2, rows, 3, 1, d)


def _to_bf16_kernel(w_ref, o_ref):
    o_ref[...] = w_ref[...].astype(BF16)


def _to_bf16(w):
    shape = w.shape
    w2 = w.reshape(-1, shape[-1])
    tr = 512
    out = pl.pallas_call(
        _to_bf16_kernel,
        grid=(w2.shape[0] // tr,),
        in_specs=[pl.BlockSpec((tr, shape[-1]), lambda i: (i, 0))],
        out_specs=pl.BlockSpec((tr, shape[-1]), lambda i: (i, 0)),
        out_shape=jax.ShapeDtypeStruct(w2.shape, BF16),
        compiler_params=_params(("arbitrary",)),
        name="to_bf16",
    )(w2)
    return out.reshape(shape)


def _neg_expm1_nonpos(v, exp_v):
    poly = 1.0 / 120.0
    for coef in (1.0 / 24.0, 1.0 / 6.0, 0.5, 1.0):
        poly = poly * v + coef
    return jnp.where(v > -0.01, -v * poly, 1.0 - exp_v)


def _lru_scan(a, u, h_carry, cols, ts):
    row = lax.broadcasted_iota(jnp.int32, (ts, 1), 0) % SUBLANES
    d = 1
    while d < SUBLANES:
        keep = row >= d
        a_prev = jnp.where(keep, pltpu.roll(a, d, 0), 1.0)
        u_prev = jnp.where(keep, pltpu.roll(u, d, 0), 0.0)
        u = u + a * u_prev
        a = a * a_prev
        d *= 2
    carry = h_carry[0:1, cols]
    groups = []
    for g in range(ts // SUBLANES):
        rows = slice(g * SUBLANES, (g + 1) * SUBLANES)
        blk = a[rows] * carry + u[rows]
        groups.append(blk)
        carry = blk[SUBLANES - 1:SUBLANES]
    h_carry[:, cols] = jnp.broadcast_to(carry, (h_carry.shape[0], carry.shape[1]))
    return jnp.concatenate(groups, axis=0)


def _lru_head(z_ref, small, y_ref, conv_buf, h_carry, hd, ts):
    cw_ref, cb_ref, wa_ref, ba_ref, wx_ref, bx_ref, lam_ref = small[:7]
    d_lru = cw_ref.shape[-1]
    hdim = wa_ref.shape[-1]
    cols = slice(hd * hdim, (hd + 1) * hdim)

    xl = z_ref[:, cols]
    conv_buf[SUBLANES:, cols] = xl
    cw = cw_ref[:, cols]
    xc = cb_ref[:, cols] + cw[CONV_WIDTH - 1:CONV_WIDTH] * xl
    for k in range(CONV_WIDTH - 1):
        back = CONV_WIDTH - 1 - k
        xc = xc + cw[k:k + 1] * conv_buf[SUBLANES - back:SUBLANES - back + ts, cols]
    conv_buf[:SUBLANES, cols] = xl[ts - SUBLANES:]

    xcb = xc.astype(BF16)
    gate_a = jax.nn.sigmoid(
        jnp.dot(xcb, wa_ref[hd], preferred_element_type=F32) + ba_ref[:, cols])
    gate_x = jax.nn.sigmoid(
        jnp.dot(xcb, wx_ref[hd], preferred_element_type=F32) + bx_ref[:, cols])
    log_a = (-LRU_C) * gate_a * jax.nn.softplus(-lam_ref[:, cols])
    a = jnp.exp(log_a)
    m2 = _neg_expm1_nonpos(2.0 * log_a, a * a)
    mult = jnp.where(m2 > 0.0, m2 * lax.rsqrt(m2), 0.0)
    u = (xc * gate_x) * mult
    hs = _lru_scan(a, u, h_carry, cols, ts)
    gl = z_ref[:, d_lru + hd * hdim:d_lru + (hd + 1) * hdim]
    y_ref[:, cols] = (hs * jax.nn.gelu(gl)).astype(y_ref.dtype)


def _pool_group(z_ref, small, y_ref, pool_hist, gi, t, ts):
    d_lru = small[0].shape[-1]
    pw_ref, pb_ref, ps_ref = small[7:]
    gd = pw_ref.shape[-1]
    w = POOL_WINDOWS[gi]
    cols = slice(gi * gd, (gi + 1) * gd)
    row = lax.broadcasted_iota(jnp.int32, (ts, 1), 0)
    xp = z_ref[:, 2 * d_lru + gi * gd:2 * d_lru + (gi + 1) * gd]
    s = jnp.concatenate([pool_hist[:, cols], xp], axis=0)
    pool_hist[:, cols] = xp[ts - POOL_HIST:]
    sh = 1
    while sh < w:
        s = s + pltpu.roll(s, sh, 0)
        sh *= 2
    cnt = jnp.minimum(t * ts + row + 1, w).astype(F32)
    p = (s[POOL_HIST:] / cnt - xp).astype(BF16)
    yp = (jnp.dot(p, pw_ref[gi], preferred_element_type=F32) + pb_ref[:, cols]) * ps_ref[:, cols]
    y_ref[:, d_lru + gi * gd:d_lru + (gi + 1) * gd] = yp.astype(y_ref.dtype)


def _mixer_kernel(x_ref, shift_ref, scale_ref, w_ref, *rest, ts, nt):
    small = rest[:10]
    y_ref, z_even, z_odd, h_mod, conv_buf, pool_hist, h_carry = rest[10:]
    t = pl.program_id(1)
    n_parts = LRU_HEADS
    assert len(POOL_WINDOWS) == n_parts
    slab = w_ref.shape[1] // n_parts

    def modulate():
        h_mod[...] = (x_ref[...] * (1.0 + scale_ref[...]) + shift_ref[...]).astype(BF16)

    def part(i, z_dst, z_src):
        if z_dst is not None:
            cols = slice(i * slab, (i + 1) * slab)
            z_dst[:, cols] = jnp.dot(h_mod[...], w_ref[:, cols], preferred_element_type=F32)
        if z_src is not None:
            _lru_head(z_src, small, y_ref, conv_buf, h_carry, i, ts)
            _pool_group(z_src, small, y_ref, pool_hist, i, t - 1, ts)

    def parts(z_dst, z_src):
        for i in range(n_parts):
            part(i, z_dst, z_src)

    @pl.when(t == 0)
    def _():
        conv_buf[...] = jnp.zeros_like(conv_buf)
        pool_hist[...] = jnp.zeros_like(pool_hist)
        h_carry[...] = jnp.zeros_like(h_carry)
        modulate()
        parts(z_even, None)

    inner = jnp.logical_and(t > 0, t < nt)

    @pl.when(jnp.logical_and(inner, t % 2 == 1))
    def _():
        modulate()
        parts(z_odd, z_even)

    @pl.when(jnp.logical_and(inner, t % 2 == 0))
    def _():
        modulate()
        parts(z_even, z_odd)

    @pl.when(t == nt)
    def _():
        parts(None, z_odd if (nt - 1) % 2 == 1 else z_even)


def _mixer(x2, mods, l, w_in_bf16, small, nb, seq):
    n, d = x2.shape
    d_in = w_in_bf16.shape[-1]
    d_lru = small[0].shape[-1]
    assert d_in == 3 * d_lru
    ts = 256
    nt = seq // ts
    batch_of = lambda b, t: b
    return pl.pallas_call(
        functools.partial(_mixer_kernel, ts=ts, nt=nt),
        grid=(nb, nt + 1),
        in_specs=[
            pl.BlockSpec((ts, d), lambda b, t: (b * nt + jnp.minimum(t, nt - 1), 0)),
            _mod_spec(mods, l, 0, SHIFT, batch_of),
            _mod_spec(mods, l, 0, SCALE, batch_of),
            pl.BlockSpec((None, d, d_in), lambda b, t: (l, 0, 0), pipeline_mode=pl.Buffered(1)),
        ] + [_layer_spec(a, l) for a in small],
        out_specs=pl.BlockSpec((ts, 2 * d_lru), lambda b, t: (b * nt + jnp.maximum(t - 1, 0), 0)),
        out_shape=jax.ShapeDtypeStruct((n, 2 * d_lru), BF16),
        scratch_shapes=[pltpu.VMEM((ts, d_in), F32), pltpu.VMEM((ts, d_in), F32),
                        pltpu.VMEM((ts, d), BF16),
                        pltpu.VMEM((SUBLANES + ts, d_lru), F32),
                        pltpu.VMEM((POOL_HIST, d_lru), F32),
                        pltpu.VMEM((SUBLANES, d_lru), F32)],
        compiler_params=_params(("arbitrary", "arbitrary")),
        name="mixer",
    )(x2, mods, mods, w_in_bf16, *small)


def _resid_ln(x, o, gate, g, b, alpha):
    v = alpha * x + (1.0 + gate) * o
    mu = jnp.mean(v, axis=-1, keepdims=True)
    dv = v - mu
    var = jnp.mean(dv * dv, axis=-1, keepdims=True)
    return dv * lax.rsqrt(var + LN_EPS) * g + b


def _route(h, wr_ref):
    tm = h.shape[0]
    h_hi = h.astype(BF16)
    h_lo = (h - h_hi.astype(F32)).astype(BF16)
    r = jnp.dot(jnp.concatenate([h_hi, h_lo], axis=0), wr_ref[...], preferred_element_type=F32)
    r = r[:tm] + r[tm:]
    logits = r + pltpu.roll(r, LANES - N_EXPERTS, 1)
    lane = lax.broadcasted_iota(jnp.int32, logits.shape, 1).astype(F32)
    neg = jnp.float32(-jnp.inf)
    lg = jnp.where(lane < N_EXPERTS, logits, neg)
    m1 = jnp.max(lg, axis=-1, keepdims=True)
    i1 = jnp.min(jnp.where(lg == m1, lane, float(LANES)), axis=-1, keepdims=True)
    lg2 = jnp.where(lane == i1, neg, lg)
    m2 = jnp.max(lg2, axis=-1, keepdims=True)
    i2 = jnp.min(jnp.where(lg2 == m2, lane, float(LANES)), axis=-1, keepdims=True)
    e = jnp.exp(m2 - m1)
    p1 = 1.0 / (1.0 + e)
    p2 = e / (1.0 + e)
    idx = jnp.where(lane == 0, i1, i2).astype(jnp.int32)
    return idx, jnp.where(lane == 0, p1, p2)


def _router_weights(router_w):
    hi = router_w.astype(BF16)
    lo = (router_w - hi.astype(F32)).astype(BF16)
    wr = jnp.concatenate([hi, lo], axis=1)
    return jnp.pad(wr, ((0, 0), (0, LANES - wr.shape[1])))


def _mix_out_kernel(y_ref, x_ref, gate_ref, g_ref, b_ref, shift_ref, scale_ref, w_ref, *rest,
                    alpha, route):
    if route:
        wr_ref, xo_ref, h_ref, idx_ref, p_ref, w_scr = rest
    else:
        xo_ref, h_ref, w_scr = rest

    @pl.when(pl.program_id(0) == 0)
    def _():
        w_scr[...] = w_ref[...].astype(BF16)

    o = jnp.dot(y_ref[...], w_scr[...], preferred_element_type=F32)
    xn = _resid_ln(x_ref[...], o, gate_ref[...], g_ref[...], b_ref[...], alpha)
    xo_ref[...] = xn
    h = xn * (1.0 + scale_ref[...]) + shift_ref[...]
    h_ref[...] = h.astype(h_ref.dtype)
    if route:
        idx, p = _route(h, wr_ref)
        idx_ref[...] = idx
        p_ref[...] = p


def _mix_out(y, x2, mods, l, ln_g, ln_b, w_out_all, router_w, seq, alpha):
    n, d = x2.shape
    route = router_w is not None
    tm = 256
    per_b = seq // tm
    batch_of = lambda i: i // per_b
    tok = lambda: pl.BlockSpec((tm, d), lambda i: (i, 0))
    vec = lambda k: pl.BlockSpec((None, None, 1, d), lambda i: (l, k, 0, 0))
    ln_g4 = ln_g.reshape(ln_g.shape[0], 2, 1, d)
    ln_b4 = ln_b.reshape(ln_b.shape[0], 2, 1, d)
    in_specs = [tok(), tok(), _mod_spec(mods, l, 0, GATE, batch_of), vec(0), vec(0),
                _mod_spec(mods, l, 1, SHIFT, batch_of), _mod_spec(mods, l, 1, SCALE, batch_of),
                pl.BlockSpec((None, d, d), lambda i: (l, 0, 0), pipeline_mode=pl.Buffered(1))]
    args = [y, x2, mods, ln_g4, ln_b4, mods, mods, w_out_all]
    out_specs = [tok(), tok()]
    out_shape = [jax.ShapeDtypeStruct((n, d), F32),
                 jax.ShapeDtypeStruct((n, d), F32 if route else BF16)]
    if route:
        in_specs.append(pl.BlockSpec((d, LANES), lambda i: (0, 0)))
        args.append(_router_weights(router_w))
        out_specs += [pl.BlockSpec((tm, LANES), lambda i: (i, 0))] * 2
        out_shape += [jax.ShapeDtypeStruct((n, LANES), jnp.int32),
                      jax.ShapeDtypeStruct((n, LANES), F32)]
    return pl.pallas_call(
        functools.partial(_mix_out_kernel, alpha=alpha, route=route),
        grid=(n // tm,),
        in_specs=in_specs,
        out_specs=out_specs,
        out_shape=out_shape,
        scratch_shapes=[pltpu.VMEM((d, d), BF16)],
        compiler_params=_params(("arbitrary",)),
        name="mix_out",
    )(*args)


def _ffn_kernel(ge_ref, gc_ref, gb_ref, gv_ref, gm_ref, x_ref, wg0_ref, wu0_ref, wd0_ref, wgn_ref,
                wun_ref, wdn_ref, *rest, ch, alpha, ln, static_modes):
    if ln:
        xres_hbm, gate_ref, g_ref, b_ref, o_ref = rest[:5]
        bufs, (xr_buf, xr_sem) = rest[5:11], rest[11:]
    else:
        o_ref, bufs = rest[0], rest[1:7]
    w_bf16 = (bufs[:3], bufs[3:])
    s = pl.program_id(0)
    j = pl.program_id(1)
    nj = pl.num_programs(1)
    n = gc_ref[s]
    cap = x_ref.shape[0]

    @pl.when(j == 0)
    def _():
        o_ref[...] = jnp.zeros_like(o_ref)

    @pl.when(jnp.logical_and(s == 0, j == 0))
    def _():
        for dst, src in zip(w_bf16[0], (wg0_ref, wu0_ref, wd0_ref)):
            dst[...] = src[...].astype(BF16)

    def step(parity, mode):
        wg_s, wu_s, wd_s = w_bf16[parity]
        static_k, rows_per_chunk = (None, ch) if mode is None else mode

        def chunk_rows(c):
            start = c * rows_per_chunk
            return pl.ds(start if isinstance(c, int) else pl.multiple_of(start, ch),
                         rows_per_chunk)

        def up(c):
            x = x_ref[chunk_rows(c), :]
            g = jnp.dot(x, wg_s[...], preferred_element_type=F32)
            u = jnp.dot(x, wu_s[...], preferred_element_type=F32)
            return (g * jax.nn.sigmoid(g) * u).astype(BF16)

        def down(c, hmid):
            o_ref[chunk_rows(c), :] += jnp.dot(hmid, wd_s[...], preferred_element_type=F32)

        def fused(c, hmid):
            down(c - 1, hmid)
            return up(c)

        def convert_next(piece, n_pieces):
            for dst, src in zip(w_bf16[1 - parity], (wgn_ref, wun_ref, wdn_ref)):
                rows = src.shape[0] // n_pieces
                sl = slice(piece * rows, (piece + 1) * rows)
                dst[sl, :] = src[sl, :].astype(BF16)

        if static_k is not None and static_k >= 3:
            n_pieces = min(4, static_k - 2)
            pending = [up(0), up(1)]
            for c in range(2, static_k):
                down(c - 2, pending.pop(0))
                pending.append(up(c))
                if c - 2 < n_pieces:
                    convert_next(c - 2, n_pieces)
            down(static_k - 2, pending.pop(0))
            down(static_k - 1, pending.pop(0))
            return

        hmid = up(0)
        if static_k is not None:
            convert_next(0, 1)
            for c in range(1, static_k):
                hmid = fused(c, hmid)
            down(static_k - 1, hmid)
        else:
            convert_next(0, 1)
            pairs = lax.div(n - 1, 2)

            def two(p, hmid):
                return fused(2 * p + 2, fused(2 * p + 1, hmid))

            hmid = lax.fori_loop(0, pairs, two, hmid)
            hmid = lax.fori_loop(2 * pairs + 1, n, fused, hmid)
            down(n - 1, hmid)

    form = gm_ref[s]
    for parity in (0, 1):
        mine = jnp.logical_and(j % 2 == parity, n > 0)
        for i, mode in enumerate(static_modes):
            pl.when(jnp.logical_and(mine, form == i + 1))(functools.partial(step, parity, mode))
        pl.when(jnp.logical_and(mine, form == 0))(functools.partial(step, parity, None))

    if ln:
        row0 = gb_ref[s] * cap

        def xres_copy(c, slot):
            return pltpu.make_async_copy(
                xres_hbm.at[pl.ds(row0 + c * ch, ch)], xr_buf.at[slot], xr_sem.at[slot])

        @pl.when(jnp.logical_and(j == nj - 1, n > 0))
        def _():
            xres_copy(0, 0).start()

            def fin(c, carry):
                slot = c % 2

                @pl.when(c + 1 < n)
                def _():
                    xres_copy(c + 1, 1 - slot).start()

                xres_copy(c, slot).wait()
                rows = pl.ds(pl.multiple_of(c * ch, ch), ch)
                o_ref[rows, :] = _resid_ln(xr_buf[slot], o_ref[rows, :],
                                           gate_ref[...], g_ref[...], b_ref[...], alpha)
                return carry

            lax.fori_loop(0, n, fin, 0)


def _ffn(x_rows, w_gate, w_up, w_down, group_expert, group_chunks, group_block, groups_used,
         group_form, *, cap, tf, ch, static_modes, ln_args=None, seq=None, alpha=None):
    p_rows, d = x_rows.shape
    f = w_gate.shape[-1]
    n_groups = group_expert.shape[0]
    nj = f // tf
    assert nj % 2 == 0
    ln = ln_args is not None

    def following(s, j, ge, gv):
        wrap = j + 1 >= nj
        s2 = jnp.where(wrap, s + 1, s)
        j2 = jnp.where(wrap, 0, j + 1)
        live = s2 < gv[0]
        last = gv[0] - 1
        return ge[jnp.where(live, s2, last)], jnp.where(live, j2, nj - 1)

    def first(s, j, ge, gc, gb, gv, gm):
        return ge[0], 0, 0

    def next_up(s, j, ge, gc, gb, gv, gm):
        e, j2 = following(s, j, ge, gv)
        return e, 0, j2

    def next_down(s, j, ge, gc, gb, gv, gm):
        e, j2 = following(s, j, ge, gv)
        return e, j2, 0

    once = pl.Buffered(1)
    in_specs = [
        pl.BlockSpec((cap, d), lambda s, j, ge, gc, gb, gv, gm: (gb[s], 0), pipeline_mode=once),
        pl.BlockSpec((None, d, tf), first, pipeline_mode=once),
        pl.BlockSpec((None, d, tf), first, pipeline_mode=once),
        pl.BlockSpec((None, tf, d), first, pipeline_mode=once),
        pl.BlockSpec((None, d, tf), next_up),
        pl.BlockSpec((None, d, tf), next_up),
        pl.BlockSpec((None, tf, d), next_down),
    ]
    args = [x_rows, w_gate, w_up, w_down, w_gate, w_up, w_down]
    scratch = 2 * [pltpu.VMEM((d, tf), BF16), pltpu.VMEM((d, tf), BF16), pltpu.VMEM((tf, d), BF16)]
    if ln:
        xres, mods, l, ln_g, ln_b = ln_args
        assert seq % cap == 0
        per_b = seq // cap
        ln_g4 = ln_g.reshape(ln_g.shape[0], 2, 1, d)
        ln_b4 = ln_b.reshape(ln_b.shape[0], 2, 1, d)
        vec = lambda: pl.BlockSpec((None, None, 1, d), lambda s, j, *tables: (l, 1, 0, 0))
        in_specs += [
            pl.BlockSpec(memory_space=pl.ANY),
            _mod_spec(mods, l, 1, GATE, lambda s, j, ge, gc, gb, gv, gm: gb[s] // per_b),
            vec(), vec(),
        ]
        args += [xres, mods, ln_g4, ln_b4]
        scratch += [pltpu.VMEM((2, ch, d), F32), pltpu.SemaphoreType.DMA((2,))]
    grid_spec = pltpu.PrefetchScalarGridSpec(
        num_scalar_prefetch=5,
        grid=(n_groups, nj),
        in_specs=in_specs,
        out_specs=pl.BlockSpec((cap, d), lambda s, j, *tables: (s, 0), pipeline_mode=once),
        scratch_shapes=scratch,
    )
    return pl.pallas_call(
        functools.partial(_ffn_kernel, ch=ch, alpha=alpha, ln=ln, static_modes=static_modes),
        grid_spec=grid_spec,
        out_shape=jax.ShapeDtypeStruct((p_rows, d), F32),
        compiler_params=_params(("arbitrary", "arbitrary")),
        name="ffn_ln" if ln else "ffn_moe",
    )(group_expert, group_chunks, group_block, groups_used, group_form, *args)


def _row_copy(src_hbm, src_row, dst_buf, dst_row, sem):
    return pltpu.make_async_copy(src_hbm.at[pl.ds(src_row, 1)], dst_buf.at[pl.ds(dst_row, 1)], sem)


def _next_tile_spec(n_tiles, width):
    return pl.BlockSpec((None, 1, width), lambda i, *_: (jnp.minimum(i + 1, n_tiles - 1), 0, 0),
                        memory_space=pltpu.SMEM)


def _first_tile_spec(width):
    return pl.BlockSpec((None, 1, width), lambda i, *_: (0, 0, 0), memory_space=pltpu.SMEM)


def _gather_kernel(cnt_ref, tok0_ref, tokn_ref, h_hbm, o_ref, buf, sem, *, tg):
    i = pl.program_id(0)
    n = pl.num_programs(0)

    def bursts(tile):
        return lax.div(cnt_ref[tile], ROW_BURST)

    def issue(tok_ref, tile, slot):
        def body(g, carry):
            for k in range(ROW_BURST):
                r = g * ROW_BURST + k
                _row_copy(h_hbm, tok_ref[0, r], buf.at[slot], r, sem.at[slot]).start()
            return carry
        lax.fori_loop(0, bursts(tile), body, 0)

    def drain(tile, slot):
        def body(g, carry):
            for k in range(ROW_BURST):
                _row_copy(h_hbm, 0, buf.at[slot], g * ROW_BURST + k, sem.at[slot]).wait()
            return carry
        lax.fori_loop(0, bursts(tile), body, 0)

    @pl.when(i == 0)
    def _():
        buf[...] = jnp.zeros_like(buf)
        issue(tok0_ref, 0, 0)

    @pl.when(i + 1 < n)
    def _():
        issue(tokn_ref, i + 1, (i + 1) % 2)

    drain(i, i % 2)
    o_ref[...] = buf[i % 2].astype(o_ref.dtype)


def _gather(h, tok_of_row, tile_rows, p_rows):
    n, d = h.shape
    tg = GATHER_TILE
    n_tiles = p_rows // tg
    tok3 = tok_of_row.reshape(n_tiles, 1, tg)
    grid_spec = pltpu.PrefetchScalarGridSpec(
        num_scalar_prefetch=1,
        grid=(n_tiles,),
        in_specs=[_first_tile_spec(tg), _next_tile_spec(n_tiles, tg),
                  pl.BlockSpec(memory_space=pl.ANY)],
        out_specs=pl.BlockSpec((tg, d), lambda i, cnt: (i, 0)),
        scratch_shapes=[pltpu.VMEM((2, tg, d), F32), pltpu.SemaphoreType.DMA((2,))],
    )
    return pl.pallas_call(
        functools.partial(_gather_kernel, tg=tg),
        grid_spec=grid_spec,
        out_shape=jax.ShapeDtypeStruct((p_rows, d), BF16),
        compiler_params=_params(("arbitrary",)),
        name="gather",
    )(tile_rows, tok3, tok3, h)


def _combine_kernel(dst0_ref, dstn_ref, x_ref, p_ref, gate_ref, g_ref, b_ref, ys_hbm, o_ref,
                    buf, sem, *, tc, alpha):
    i = pl.program_id(0)
    n = pl.num_programs(0)

    def issue(dst_ref, slot):
        def body(r, carry):
            for k in range(TOP_K):
                _row_copy(ys_hbm, dst_ref[0, r * TOP_K + k], buf.at[slot, k], r,
                          sem.at[slot]).start()
            return carry
        lax.fori_loop(0, tc, body, 0, unroll=4)

    def drain(slot):
        def body(r, carry):
            for k in range(TOP_K):
                _row_copy(ys_hbm, 0, buf.at[slot, k], r, sem.at[slot]).wait()
            return carry
        lax.fori_loop(0, tc, body, 0, unroll=4)

    @pl.when(i == 0)
    def _():
        issue(dst0_ref, 0)

    @pl.when(i + 1 < n)
    def _():
        issue(dstn_ref, (i + 1) % 2)

    slot = i % 2
    drain(slot)
    p = p_ref[...]
    o = p[:, 0:1] * buf[slot, 0] + p[:, 1:2] * buf[slot, 1]
    o_ref[...] = _resid_ln(x_ref[...], o, gate_ref[...], g_ref[...], b_ref[...], alpha)


def _combine(ys, dest, x2, probs, mods, l, ln_g, ln_b, seq, alpha):
    n, d = x2.shape
    tc = 256
    n_tiles = n // tc
    per_b = seq // tc
    dest3 = dest.reshape(n_tiles, 1, tc * TOP_K)
    ln_g4 = ln_g.reshape(ln_g.shape[0], 2, 1, d)
    ln_b4 = ln_b.reshape(ln_b.shape[0], 2, 1, d)
    vec = lambda: pl.BlockSpec((None, None, 1, d), lambda i: (l, 1, 0, 0))
    return pl.pallas_call(
        functools.partial(_combine_kernel, tc=tc, alpha=alpha),
        grid=(n_tiles,),
        in_specs=[
            _first_tile_spec(tc * TOP_K), _next_tile_spec(n_tiles, tc * TOP_K),
            pl.BlockSpec((tc, d), lambda i: (i, 0)),
            pl.BlockSpec((tc, LANES), lambda i: (i, 0)),
            _mod_spec(mods, l, 1, GATE, lambda i: i // per_b),
            vec(), vec(),
            pl.BlockSpec(memory_space=pl.ANY),
        ],
        out_specs=pl.BlockSpec((tc, d), lambda i: (i, 0)),
        out_shape=jax.ShapeDtypeStruct((n, d), F32),
        scratch_shapes=[pltpu.VMEM((2, TOP_K, tc, d), F32), pltpu.SemaphoreType.DMA((2,))],
        compiler_params=_params(("arbitrary",)),
        name="combine",
    )(dest3, dest3, x2, probs, mods, ln_g4, ln_b4, ys)


def _routing_tables(top_i, cap, ch, static_modes, max_groups):
    n = top_i.shape[0]
    e_flat = top_i.reshape(-1)
    onehot = (e_flat[:, None] == jnp.arange(N_EXPERTS, dtype=jnp.int32)[None, :]).astype(jnp.int32)
    csum = jnp.cumsum(onehot, axis=0)
    counts = csum[-1]
    rank = jnp.take_along_axis(csum, e_flat[:, None], axis=1)[:, 0] - 1
    groups_per = (counts + cap - 1) // cap
    group_end = jnp.cumsum(groups_per)
    group_start = group_end - groups_per
    dest = (group_start * cap)[e_flat] + rank
    n_groups = group_end[-1]
    gidx = jnp.arange(max_groups, dtype=jnp.int32)
    gclamp = jnp.minimum(gidx, n_groups - 1)
    group_expert = jnp.sum((group_end[None, :] <= gclamp[:, None]).astype(jnp.int32), axis=1)
    rows = jnp.clip(counts[group_expert] - (gclamp - group_start[group_expert]) * cap, 0, cap)
    group_chunks = jnp.where(gidx < n_groups, (rows + ch - 1) // ch, 0)
    group_form = jnp.zeros_like(group_chunks)
    for i, (k, rpc) in reversed(list(enumerate(static_modes))):
        fits = jnp.logical_and(rows <= k * rpc, rows > (k - 1) * ch)
        group_form = jnp.where(fits, i + 1, group_form)
    p_rows = max_groups * cap
    tok_of_row = (jnp.arange(p_rows, dtype=jnp.int32) % n).at[dest].set(
        jnp.arange(n * TOP_K, dtype=jnp.int32) // TOP_K)
    tiles_per_group = cap // GATHER_TILE
    in_group = jnp.where(gidx < n_groups, rows, 0)[:, None] - (
        jnp.arange(tiles_per_group, dtype=jnp.int32) * GATHER_TILE)[None, :]
    tile_rows = jnp.clip(in_group, 0, GATHER_TILE).reshape(-1)
    tile_rows = (tile_rows + ROW_BURST - 1) // ROW_BURST * ROW_BURST
    return (dest.astype(jnp.int32), tok_of_row, tile_rows.astype(jnp.int32),
            group_expert.astype(jnp.int32), group_chunks.astype(jnp.int32), gclamp.astype(jnp.int32),
            n_groups.reshape(1).astype(jnp.int32), group_form.astype(jnp.int32))


def kernel(x, c, ada_w, ada_b, ln_g, ln_b, mix_w_in, conv_w, conv_b, lru_wa, lru_ba, lru_wx, lru_bx,
           lru_lam, pool_w, pool_b, pool_scale, mix_w_out, ffn_w_gate, ffn_w_up, ffn_w_down,
           router_w, exp_w_gate, exp_w_up, exp_w_down):
    nb, seq, d = x.shape
    depth = ada_w.shape[0]
    n = nb * seq
    alpha = float((2 * depth) ** 0.25)
    mods = _ada(c, ada_w, ada_b)
    x2 = x.reshape(n, d)

    row3 = lambda v: v.reshape(depth, 1, -1)
    seq_params = [conv_w, row3(conv_b), lru_wa.astype(BF16), row3(lru_ba), lru_wx.astype(BF16),
                  row3(lru_bx), row3(lru_lam), pool_w.astype(BF16), row3(pool_b), row3(pool_scale)]

    w_in_bf16 = _to_bf16(mix_w_in)

    ffn_tf, ffn_ch = 256, 256
    dense_cap = seq
    moe_cap = 2304
    moe_modes = ((8, 256), (8, 272))
    for l in range(depth):
        moe = (l % 2 == 1)
        i = l // 2

        y = _mixer(x2, mods, l, w_in_bf16, seq_params, nb, seq)
        outs = _mix_out(y, x2, mods, l, ln_g, ln_b, mix_w_out, router_w[i] if moe else None,
                        seq, alpha)
        if not moe:
            x2, h = outs
            n_groups = n // dense_cap
            ge = jnp.full((n_groups,), i, jnp.int32)
            gc = jnp.full((n_groups,), dense_cap // ffn_ch, jnp.int32)
            gb = jnp.arange(n_groups, dtype=jnp.int32)
            gv = jnp.full((1,), n_groups, jnp.int32)
            gm = jnp.ones((n_groups,), jnp.int32)
            x2 = _ffn(h, ffn_w_gate, ffn_w_up, ffn_w_down, ge, gc, gb, gv, gm,
                      cap=dense_cap, tf=ffn_tf, ch=ffn_ch,
                      static_modes=((dense_cap // ffn_ch, ffn_ch),),
                      ln_args=(x2, mods, l, ln_g, ln_b), seq=seq, alpha=alpha)
        else:
            x_mix, h, idx, probs = outs
            top_i = idx[:, :TOP_K]
            n_exp = exp_w_gate.shape[1]
            f_exp = exp_w_gate.shape[-1]

            def moe(max_groups):
                dest, tok_of_row, tile_rows, ge, gc, gb, gv, gm = _routing_tables(
                    top_i, moe_cap, ffn_ch, moe_modes, max_groups)
                xs = _gather(h, tok_of_row, tile_rows, max_groups * moe_cap)
                ys = _ffn(xs,
                          exp_w_gate.reshape(-1, d, f_exp), exp_w_up.reshape(-1, d, f_exp),
                          exp_w_down.reshape(-1, f_exp, d), ge + i * n_exp, gc, gb, gv, gm,
                          cap=moe_cap, tf=ffn_tf, ch=ffn_ch, static_modes=moe_modes)
                return _combine(ys, dest, x_mix, probs, mods, l, ln_g, ln_b, seq, alpha)

            x2 = moe((n * TOP_K) // moe_cap + n_exp)
    return x2.reshape(nb, seq, d)
```

```python
import functools

import jax
import jax.numpy as jnp
from jax import lax
from jax.experimental import pallas as pl
from jax.experimental.pallas import tpu as pltpu

F32 = jnp.float32
BF16 = jnp.bfloat16

LN_EPS = 1e-5
LRU_C = 8.0
CONV_WIDTH = 4
LRU_HEADS = 4
POOL_WINDOWS = (2, 4, 8, 16)
N_EXPERTS = 8
TOP_K = 2

V7X_VMEM_LIMIT_BYTES = 58 * 1024 * 1024
LANES = 128
SUBLANES = 8
POOL_HIST = 16
SHIFT, SCALE, GATE = 0, 1, 2
GATHER_TILE = 768
ROW_BURST = 8


def _params(sem, vmem=V7X_VMEM_LIMIT_BYTES):
    return pltpu.CompilerParams(dimension_semantics=sem, vmem_limit_bytes=vmem)


def _mod_spec(mods, l, k, which, batch_of):
    d = mods.shape[-1]
    return pl.BlockSpec((None, None, None, None, 1, d),
                        lambda *g: (l, k, batch_of(*g), which, 0, 0))


def _layer_spec(arr, l):
    nd = arr.ndim - 1
    return pl.BlockSpec((None,) + arr.shape[1:], lambda *g: (l,) + (0,) * nd)


def _ada_kernel(c_ref, w_ref, b_ref, o_ref):
    c = c_ref[...]
    c_act = (c * jax.nn.sigmoid(c)).astype(BF16)
    o_ref[...] = jnp.dot(c_act, w_ref[...].astype(BF16),
                         preferred_element_type=F32) + b_ref[...]


def _ada(c, ada_w, ada_b):
    depth, _, d, d3 = ada_w.shape
    nb = c.shape[0]
    rows = -(-nb // SUBLANES) * SUBLANES
    c_pad = jnp.pad(c, ((0, rows - nb), (0, 0)))
    w = ada_w.reshape(depth * 2, d, d3)
    b = ada_b.reshape(depth * 2, 1, d3)
    tn = 1024
    out = pl.pallas_call(
        _ada_kernel,
        grid=(depth * 2, d3 // tn),
        in_specs=[
            pl.BlockSpec((rows, d), lambda l, j: (0, 0)),
            pl.BlockSpec((None, d, tn), lambda l, j: (l, 0, j)),
            pl.BlockSpec((None, 1, tn), lambda l, j: (l, 0, j)),
        ],
        out_specs=pl.BlockSpec((None, rows, tn), lambda l, j: (l, 0, j)),
        out_shape=jax.ShapeDtypeStruct((depth * 2, rows, d3), F32),
        compiler_params=_params(("arbitrary", "arbitrary")),
        name="ada",
    )(c_pad, w, b)
    return out.reshape(depth, 2, rows, 3, 1, d)


def _to_bf16_kernel(w_ref, o_ref):
    o_ref[...] = w_ref[...].astype(BF16)


def _to_bf16(w):
    shape = w.shape
    w2 = w.reshape(-1, shape[-1])
    tr = 512
    out = pl.pallas_call(
        _to_bf16_kernel,
        grid=(w2.shape[0] // tr,),
        in_specs=[pl.BlockSpec((tr, shape[-1]), lambda i: (i, 0))],
        out_specs=pl.BlockSpec((tr, shape[-1]), lambda i: (i, 0)),
        out_shape=jax.ShapeDtypeStruct(w2.shape, BF16),
        compiler_params=_params(("arbitrary",)),
        name="to_bf16",
    )(w2)
    return out.reshape(shape)


def _neg_expm1_nonpos(v, exp_v):
    poly = 1.0 / 120.0
    for coef in (1.0 / 24.0, 1.0 / 6.0, 0.5, 1.0):
        poly = poly * v + coef
    return jnp.where(v > -0.01, -v * poly, 1.0 - exp_v)


def _lru_scan(a, u, h_carry, cols, ts):
    row = lax.broadcasted_iota(jnp.int32, (ts, 1), 0) % SUBLANES
    d = 1
    while d < SUBLANES:
        keep = row >= d
        a_prev = jnp.where(keep, pltpu.roll(a, d, 0), 1.0)
        u_prev = jnp.where(keep, pltpu.roll(u, d, 0), 0.0)
        u = u + a * u_prev
        a = a * a_prev
        d *= 2
    carry = h_carry[0:1, cols]
    groups = []
    for g in range(ts // SUBLANES):
        rows = slice(g * SUBLANES, (g + 1) * SUBLANES)
        blk = a[rows] * carry + u[rows]
        groups.append(blk)
        carry = blk[SUBLANES - 1:SUBLANES]
    h_carry[:, cols] = jnp.broadcast_to(carry, (h_carry.shape[0], carry.shape[1]))
    return jnp.concatenate(groups, axis=0)


def _lru_head(z_ref, small, y_ref, conv_buf, h_carry, hd, ts):
    cw_ref, cb_ref, wa_ref, ba_ref, wx_ref, bx_ref, lam_ref = small[:7]
    d_lru = cw_ref.shape[-1]
    hdim = wa_ref.shape[-1]
    cols = slice(hd * hdim, (hd + 1) * hdim)

    xl = z_ref[:, cols]
    conv_buf[SUBLANES:, cols] = xl
    cw = cw_ref[:, cols]
    xc = cb_ref[:, cols] + cw[CONV_WIDTH - 1:CONV_WIDTH] * xl
    for k in range(CONV_WIDTH - 1):
        back = CONV_WIDTH - 1 - k
        xc = xc + cw[k:k + 1] * conv_buf[SUBLANES - back:SUBLANES - back + ts, cols]
    conv_buf[:SUBLANES, cols] = xl[ts - SUBLANES:]

    xcb = xc.astype(BF16)
    gate_a = jax.nn.sigmoid(
        jnp.dot(xcb, wa_ref[hd], preferred_element_type=F32) + ba_ref[:, cols])
    gate_x = jax.nn.sigmoid(
        jnp.dot(xcb, wx_ref[hd], preferred_element_type=F32) + bx_ref[:, cols])
    log_a = (-LRU_C) * gate_a * jax.nn.softplus(-lam_ref[:, cols])
    a = jnp.exp(log_a)
    m2 = _neg_expm1_nonpos(2.0 * log_a, a * a)
    mult = jnp.where(m2 > 0.0, m2 * lax.rsqrt(m2), 0.0)
    u = (xc * gate_x) * mult
    hs = _lru_scan(a, u, h_carry, cols, ts)
    gl = z_ref[:, d_lru + hd * hdim:d_lru + (hd + 1) * hdim]
    y_ref[:, cols] = (hs * jax.nn.gelu(gl)).astype(y_ref.dtype)


def _pool_group(z_ref, small, y_ref, pool_hist, gi, t, ts):
    d_lru = small[0].shape[-1]
    pw_ref, pb_ref, ps_ref = small[7:]
    gd = pw_ref.shape[-1]
    w = POOL_WINDOWS[gi]
    cols = slice(gi * gd, (gi + 1) * gd)
    row = lax.broadcasted_iota(jnp.int32, (ts, 1), 0)
    xp = z_ref[:, 2 * d_lru + gi * gd:2 * d_lru + (gi + 1) * gd]
    s = jnp.concatenate([pool_hist[:, cols], xp], axis=0)
    pool_hist[:, cols] = xp[ts - POOL_HIST:]
    sh = 1
    while sh < w:
        s = s + pltpu.roll(s, sh, 0)
        sh *= 2
    cnt = jnp.minimum(t * ts + row + 1, w).astype(F32)
    p = (s[POOL_HIST:] / cnt - xp).astype(BF16)
    yp = (jnp.dot(p, pw_ref[gi], preferred_element_type=F32) + pb_ref[:, cols]) * ps_ref[:, cols]
    y_ref[:, d_lru + gi * gd:d_lru + (gi + 1) * gd] = yp.astype(y_ref.dtype)


def _mixer_kernel(x_ref, shift_ref, scale_ref, w_ref, *rest, ts, nt):
    small = rest[:10]
    y_ref, z_even, z_odd, h_mod, conv_buf, pool_hist, h_carry = rest[10:]
    t = pl.program_id(1)
    n_parts = LRU_HEADS
    assert len(POOL_WINDOWS) == n_parts
    slab = w_ref.shape[1] // n_parts

    def modulate():
        h_mod[...] = (x_ref[...] * (1.0 + scale_ref[...]) + shift_ref[...]).astype(BF16)

    def part(i, z_dst, z_src):
        if z_dst is not None:
            cols = slice(i * slab, (i + 1) * slab)
            z_dst[:, cols] = jnp.dot(h_mod[...], w_ref[:, cols], preferred_element_type=F32)
        if z_src is not None:
            _lru_head(z_src, small, y_ref, conv_buf, h_carry, i, ts)
            _pool_group(z_src, small, y_ref, pool_hist, i, t - 1, ts)

    def parts(z_dst, z_src):
        for i in range(n_parts):
            part(i, z_dst, z_src)

    @pl.when(t == 0)
    def _():
        conv_buf[...] = jnp.zeros_like(conv_buf)
        pool_hist[...] = jnp.zeros_like(pool_hist)
        h_carry[...] = jnp.zeros_like(h_carry)
        modulate()
        parts(z_even, None)

    inner = jnp.logical_and(t > 0, t < nt)

    @pl.when(jnp.logical_and(inner, t % 2 == 1))
    def _():
        modulate()
        parts(z_odd, z_even)

    @pl.when(jnp.logical_and(inner, t % 2 == 0))
    def _():
        modulate()
        parts(z_even, z_odd)

    @pl.when(t == nt)
    def _():
        parts(None, z_odd if (nt - 1) % 2 == 1 else z_even)


def _mixer(x2, mods, l, w_in_bf16, small, nb, seq):
    n, d = x2.shape
    d_in = w_in_bf16.shape[-1]
    d_lru = small[0].shape[-1]
    assert d_in == 3 * d_lru
    ts = 256
    nt = seq // ts
    batch_of = lambda b, t: b
    return pl.pallas_call(
        functools.partial(_mixer_kernel, ts=ts, nt=nt),
        grid=(nb, nt + 1),
        in_specs=[
            pl.BlockSpec((ts, d), lambda b, t: (b * nt + jnp.minimum(t, nt - 1), 0)),
            _mod_spec(mods, l, 0, SHIFT, batch_of),
            _mod_spec(mods, l, 0, SCALE, batch_of),
            pl.BlockSpec((None, d, d_in), lambda b, t: (l, 0, 0), pipeline_mode=pl.Buffered(1)),
        ] + [_layer_spec(a, l) for a in small],
        out_specs=pl.BlockSpec((ts, 2 * d_lru), lambda b, t: (b * nt + jnp.maximum(t - 1, 0), 0)),
        out_shape=jax.ShapeDtypeStruct((n, 2 * d_lru), BF16),
        scratch_shapes=[pltpu.VMEM((ts, d_in), F32), pltpu.VMEM((ts, d_in), F32),
                        pltpu.VMEM((ts, d), BF16),
                        pltpu.VMEM((SUBLANES + ts, d_lru), F32),
                        pltpu.VMEM((POOL_HIST, d_lru), F32),
                        pltpu.VMEM((SUBLANES, d_lru), F32)],
        compiler_params=_params(("arbitrary", "arbitrary")),
        name="mixer",
    )(x2, mods, mods, w_in_bf16, *small)


def _resid_ln(x, o, gate, g, b, alpha):
    v = alpha * x + (1.0 + gate) * o
    mu = jnp.mean(v, axis=-1, keepdims=True)
    dv = v - mu
    var = jnp.mean(dv * dv, axis=-1, keepdims=True)
    return dv * lax.rsqrt(var + LN_EPS) * g + b


def _route(h, wr_ref):
    tm = h.shape[0]
    h_hi = h.astype(BF16)
    h_lo = (h - h_hi.astype(F32)).astype(BF16)
    r = jnp.dot(jnp.concatenate([h_hi, h_lo], axis=0), wr_ref[...], preferred_element_type=F32)
    r = r[:tm] + r[tm:]
    logits = r + pltpu.roll(r, LANES - N_EXPERTS, 1)
    lane = lax.broadcasted_iota(jnp.int32, logits.shape, 1).astype(F32)
    neg = jnp.float32(-jnp.inf)
    lg = jnp.where(lane < N_EXPERTS, logits, neg)
    m1 = jnp.max(lg, axis=-1, keepdims=True)
    i1 = jnp.min(jnp.where(lg == m1, lane, float(LANES)), axis=-1, keepdims=True)
    lg2 = jnp.where(lane == i1, neg, lg)
    m2 = jnp.max(lg2, axis=-1, keepdims=True)
    i2 = jnp.min(jnp.where(lg2 == m2, lane, float(LANES)), axis=-1, keepdims=True)
    e = jnp.exp(m2 - m1)
    p1 = 1.0 / (1.0 + e)
    p2 = e / (1.0 + e)
    idx = jnp.where(lane == 0, i1, i2).astype(jnp.int32)
    return idx, jnp.where(lane == 0, p1, p2)


def _router_weights(router_w):
    hi = router_w.astype(BF16)
    lo = (router_w - hi.astype(F32)).astype(BF16)
    wr = jnp.concatenate([hi, lo], axis=1)
    return jnp.pad(wr, ((0, 0), (0, LANES - wr.shape[1])))


def _mix_out_kernel(y_ref, x_ref, gate_ref, g_ref, b_ref, shift_ref, scale_ref, w_ref, *rest,
                    alpha, route):
    if route:
        wr_ref, xo_ref, h_ref, idx_ref, p_ref, w_scr = rest
    else:
        xo_ref, h_ref, w_scr = rest

    @pl.when(pl.program_id(0) == 0)
    def _():
        w_scr[...] = w_ref[...].astype(BF16)

    o = jnp.dot(y_ref[...], w_scr[...], preferred_element_type=F32)
    xn = _resid_ln(x_ref[...], o, gate_ref[...], g_ref[...], b_ref[...], alpha)
    xo_ref[...] = xn
    h = xn * (1.0 + scale_ref[...]) + shift_ref[...]
    h_ref[...] = h.astype(h_ref.dtype)
    if route:
        idx, p = _route(h, wr_ref)
        idx_ref[...] = idx
        p_ref[...] = p


def _mix_out(y, x2, mods, l, ln_g, ln_b, w_out_all, router_w, seq, alpha):
    n, d = x2.shape
    route = router_w is not None
    tm = 256
    per_b = seq // tm
    batch_of = lambda i: i // per_b
    tok = lambda: pl.BlockSpec((tm, d), lambda i: (i, 0))
    vec = lambda k: pl.BlockSpec((None, None, 1, d), lambda i: (l, k, 0, 0))
    ln_g4 = ln_g.reshape(ln_g.shape[0], 2, 1, d)
    ln_b4 = ln_b.reshape(ln_b.shape[0], 2, 1, d)
    in_specs = [tok(), tok(), _mod_spec(mods, l, 0, GATE, batch_of), vec(0), vec(0),
                _mod_spec(mods, l, 1, SHIFT, batch_of), _mod_spec(mods, l, 1, SCALE, batch_of),
                pl.BlockSpec((None, d, d), lambda i: (l, 0, 0), pipeline_mode=pl.Buffered(1))]
    args = [y, x2, mods, ln_g4, ln_b4, mods, mods, w_out_all]
    out_specs = [tok(), tok()]
    out_shape = [jax.ShapeDtypeStruct((n, d), F32),
                 jax.ShapeDtypeStruct((n, d), F32 if route else BF16)]
    if route:
        in_specs.append(pl.BlockSpec((d, LANES), lambda i: (0, 0)))
        args.append(_router_weights(router_w))
        out_specs += [pl.BlockSpec((tm, LANES), lambda i: (i, 0))] * 2
        out_shape += [jax.ShapeDtypeStruct((n, LANES), jnp.int32),
                      jax.ShapeDtypeStruct((n, LANES), F32)]
    return pl.pallas_call(
        functools.partial(_mix_out_kernel, alpha=alpha, route=route),
        grid=(n // tm,),
        in_specs=in_specs,
        out_specs=out_specs,
        out_shape=out_shape,
        scratch_shapes=[pltpu.VMEM((d, d), BF16)],
        compiler_params=_params(("arbitrary",)),
        name="mix_out",
    )(*args)


def _ffn_kernel(ge_ref, gc_ref, gb_ref, gv_ref, gm_ref, x_ref, wg0_ref, wu0_ref, wd0_ref, wgn_ref,
                wun_ref, wdn_ref, *rest, ch, alpha, ln, static_modes):
    if ln:
        xres_hbm, gate_ref, g_ref, b_ref, o_ref = rest[:5]
        bufs, (xr_buf, xr_sem) = rest[5:11], rest[11:]
    else:
        o_ref, bufs = rest[0], rest[1:7]
    w_bf16 = (bufs[:3], bufs[3:])
    s = pl.program_id(0)
    j = pl.program_id(1)
    nj = pl.num_programs(1)
    n = gc_ref[s]
    cap = x_ref.shape[0]

    @pl.when(j == 0)
    def _():
        o_ref[...] = jnp.zeros_like(o_ref)

    @pl.when(jnp.logical_and(s == 0, j == 0))
    def _():
        for dst, src in zip(w_bf16[0], (wg0_ref, wu0_ref, wd0_ref)):
            dst[...] = src[...].astype(BF16)

    def step(parity, mode):
        wg_s, wu_s, wd_s = w_bf16[parity]
        static_k, rows_per_chunk = (None, ch) if mode is None else mode

        def chunk_rows(c):
            start = c * rows_per_chunk
            return pl.ds(start if isinstance(c, int) else pl.multiple_of(start, ch),
                         rows_per_chunk)

        def up(c):
            x = x_ref[chunk_rows(c), :]
            g = jnp.dot(x, wg_s[...], preferred_element_type=F32)
            u = jnp.dot(x, wu_s[...], preferred_element_type=F32)
            return (g * jax.nn.sigmoid(g) * u).astype(BF16)

        def down(c, hmid):
            o_ref[chunk_rows(c), :] += jnp.dot(hmid, wd_s[...], preferred_element_type=F32)

        def fused(c, hmid):
            down(c - 1, hmid)
            return up(c)

        def convert_next(piece, n_pieces):
            for dst, src in zip(w_bf16[1 - parity], (wgn_ref, wun_ref, wdn_ref)):
                rows = src.shape[0] // n_pieces
                sl = slice(piece * rows, (piece + 1) * rows)
                dst[sl, :] = src[sl, :].astype(BF16)

        if static_k is not None and static_k >= 3:
            n_pieces = min(4, static_k - 2)
            pending = [up(0), up(1)]
            for c in range(2, static_k):
                down(c - 2, pending.pop(0))
                pending.append(up(c))
                if c - 2 < n_pieces:
                    convert_next(c - 2, n_pieces)
            down(static_k - 2, pending.pop(0))
            down(static_k - 1, pending.pop(0))
            return

        hmid = up(0)
        if static_k is not None:
            convert_next(0, 1)
            for c in range(1, static_k):
                hmid = fused(c, hmid)
            down(static_k - 1, hmid)
        else:
            convert_next(0, 1)
            pairs = lax.div(n - 1, 2)

            def two(p, hmid):
                return fused(2 * p + 2, fused(2 * p + 1, hmid))

            hmid = lax.fori_loop(0, pairs, two, hmid)
            hmid = lax.fori_loop(2 * pairs + 1, n, fused, hmid)
            down(n - 1, hmid)

    form = gm_ref[s]
    for parity in (0, 1):
        mine = jnp.logical_and(j % 2 == parity, n > 0)
        for i, mode in enumerate(static_modes):
            pl.when(jnp.logical_and(mine, form == i + 1))(functools.partial(step, parity, mode))
        pl.when(jnp.logical_and(mine, form == 0))(functools.partial(step, parity, None))

    if ln:
        row0 = gb_ref[s] * cap

        def xres_copy(c, slot):
            return pltpu.make_async_copy(
                xres_hbm.at[pl.ds(row0 + c * ch, ch)], xr_buf.at[slot], xr_sem.at[slot])

        @pl.when(jnp.logical_and(j == nj - 1, n > 0))
        def _():
            xres_copy(0, 0).start()

            def fin(c, carry):
                slot = c % 2

                @pl.when(c + 1 < n)
                def _():
                    xres_copy(c + 1, 1 - slot).start()

                xres_copy(c, slot).wait()
                rows = pl.ds(pl.multiple_of(c * ch, ch), ch)
                o_ref[rows, :] = _resid_ln(xr_buf[slot], o_ref[rows, :],
                                           gate_ref[...], g_ref[...], b_ref[...], alpha)
                return carry

            lax.fori_loop(0, n, fin, 0)


def _ffn(x_rows, w_gate, w_up, w_down, group_expert, group_chunks, group_block, groups_used,
         group_form, *, cap, tf, ch, static_modes, ln_args=None, seq=None, alpha=None):
    p_rows, d = x_rows.shape
    f = w_gate.shape[-1]
    n_groups = group_expert.shape[0]
    nj = f // tf
    assert nj % 2 == 0
    ln = ln_args is not None

    def following(s, j, ge, gv):
        wrap = j + 1 >= nj
        s2 = jnp.where(wrap, s + 1, s)
        j2 = jnp.where(wrap, 0, j + 1)
        live = s2 < gv[0]
        last = gv[0] - 1
        return ge[jnp.where(live, s2, last)], jnp.where(live, j2, nj - 1)

    def first(s, j, ge, gc, gb, gv, gm):
        return ge[0], 0, 0

    def next_up(s, j, ge, gc, gb, gv, gm):
        e, j2 = following(s, j, ge, gv)
        return e, 0, j2

    def next_down(s, j, ge, gc, gb, gv, gm):
        e, j2 = following(s, j, ge, gv)
        return e, j2, 0

    once = pl.Buffered(1)
    in_specs = [
        pl.BlockSpec((cap, d), lambda s, j, ge, gc, gb, gv, gm: (gb[s], 0), pipeline_mode=once),
        pl.BlockSpec((None, d, tf), first, pipeline_mode=once),
        pl.BlockSpec((None, d, tf), first, pipeline_mode=once),
        pl.BlockSpec((None, tf, d), first, pipeline_mode=once),
        pl.BlockSpec((None, d, tf), next_up),
        pl.BlockSpec((None, d, tf), next_up),
        pl.BlockSpec((None, tf, d), next_down),
    ]
    args = [x_rows, w_gate, w_up, w_down, w_gate, w_up, w_down]
    scratch = 2 * [pltpu.VMEM((d, tf), BF16), pltpu.VMEM((d, tf), BF16), pltpu.VMEM((tf, d), BF16)]
    if ln:
        xres, mods, l, ln_g, ln_b = ln_args
        assert seq % cap == 0
        per_b = seq // cap
        ln_g4 = ln_g.reshape(ln_g.shape[0], 2, 1, d)
        ln_b4 = ln_b.reshape(ln_b.shape[0], 2, 1, d)
        vec = lambda: pl.BlockSpec((None, None, 1, d), lambda s, j, *tables: (l, 1, 0, 0))
        in_specs += [
            pl.BlockSpec(memory_space=pl.ANY),
            _mod_spec(mods, l, 1, GATE, lambda s, j, ge, gc, gb, gv, gm: gb[s] // per_b),
            vec(), vec(),
        ]
        args += [xres, mods, ln_g4, ln_b4]
        scratch += [pltpu.VMEM((2, ch, d), F32), pltpu.SemaphoreType.DMA((2,))]
    grid_spec = pltpu.PrefetchScalarGridSpec(
        num_scalar_prefetch=5,
        grid=(n_groups, nj),
        in_specs=in_specs,
        out_specs=pl.BlockSpec((cap, d), lambda s, j, *tables: (s, 0), pipeline_mode=once),
        scratch_shapes=scratch,
    )
    return pl.pallas_call(
        functools.partial(_ffn_kernel, ch=ch, alpha=alpha, ln=ln, static_modes=static_modes),
        grid_spec=grid_spec,
        out_shape=jax.ShapeDtypeStruct((p_rows, d), F32),
        compiler_params=_params(("arbitrary", "arbitrary")),
        name="ffn_ln" if ln else "ffn_moe",
    )(group_expert, group_chunks, group_block, groups_used, group_form, *args)


def _row_copy(src_hbm, src_row, dst_buf, dst_row, sem):
    return pltpu.make_async_copy(src_hbm.at[pl.ds(src_row, 1)], dst_buf.at[pl.ds(dst_row, 1)], sem)


def _next_tile_spec(n_tiles, width):
    return pl.BlockSpec((None, 1, width), lambda i, *_: (jnp.minimum(i + 1, n_tiles - 1), 0, 0),
                        memory_space=pltpu.SMEM)


def _first_tile_spec(width):
    return pl.BlockSpec((None, 1, width), lambda i, *_: (0, 0, 0), memory_space=pltpu.SMEM)


def _gather_kernel(cnt_ref, tok0_ref, tokn_ref, h_hbm, o_ref, buf, sem, *, tg):
    i = pl.program_id(0)
    n = pl.num_programs(0)

    def bursts(tile):
        return lax.div(cnt_ref[tile], ROW_BURST)

    def issue(tok_ref, tile, slot):
        def body(g, carry):
            for k in range(ROW_BURST):
                r = g * ROW_BURST + k
                _row_copy(h_hbm, tok_ref[0, r], buf.at[slot], r, sem.at[slot]).start()
            return carry
        lax.fori_loop(0, bursts(tile), body, 0)

    def drain(tile, slot):
        def body(g, carry):
            for k in range(ROW_BURST):
                _row_copy(h_hbm, 0, buf.at[slot], g * ROW_BURST + k, sem.at[slot]).wait()
            return carry
        lax.fori_loop(0, bursts(tile), body, 0)

    @pl.when(i == 0)
    def _():
        buf[...] = jnp.zeros_like(buf)
        issue(tok0_ref, 0, 0)

    @pl.when(i + 1 < n)
    def _():
        issue(tokn_ref, i + 1, (i + 1) % 2)

    drain(i, i % 2)
    o_ref[...] = buf[i % 2].astype(o_ref.dtype)


def _gather(h, tok_of_row, tile_rows, p_rows):
    n, d = h.shape
    tg = GATHER_TILE
    n_tiles = p_rows // tg
    tok3 = tok_of_row.reshape(n_tiles, 1, tg)
    grid_spec = pltpu.PrefetchScalarGridSpec(
        num_scalar_prefetch=1,
        grid=(n_tiles,),
        in_specs=[_first_tile_spec(tg), _next_tile_spec(n_tiles, tg),
                  pl.BlockSpec(memory_space=pl.ANY)],
        out_specs=pl.BlockSpec((tg, d), lambda i, cnt: (i, 0)),
        scratch_shapes=[pltpu.VMEM((2, tg, d), F32), pltpu.SemaphoreType.DMA((2,))],
    )
    return pl.pallas_call(
        functools.partial(_gather_kernel, tg=tg),
        grid_spec=grid_spec,
        out_shape=jax.ShapeDtypeStruct((p_rows, d), BF16),
        compiler_params=_params(("arbitrary",)),
        name="gather",
    )(tile_rows, tok3, tok3, h)


def _combine_kernel(dst0_ref, dstn_ref, x_ref, p_ref, gate_ref, g_ref, b_ref, ys_hbm, o_ref,
                    buf, sem, *, tc, alpha):
    i = pl.program_id(0)
    n = pl.num_programs(0)

    def issue(dst_ref, slot):
        def body(r, carry):
            for k in range(TOP_K):
                _row_copy(ys_hbm, dst_ref[0, r * TOP_K + k], buf.at[slot, k], r,
                          sem.at[slot]).start()
            return carry
        lax.fori_loop(0, tc, body, 0, unroll=4)

    def drain(slot):
        def body(r, carry):
            for k in range(TOP_K):
                _row_copy(ys_hbm, 0, buf.at[slot, k], r, sem.at[slot]).wait()
            return carry
        lax.fori_loop(0, tc, body, 0, unroll=4)

    @pl.when(i == 0)
    def _():
        issue(dst0_ref, 0)

    @pl.when(i + 1 < n)
    def _():
        issue(dstn_ref, (i + 1) % 2)

    slot = i % 2
    drain(slot)
    p = p_ref[...]
    o = p[:, 0:1] * buf[slot, 0] + p[:, 1:2] * buf[slot, 1]
    o_ref[...] = _resid_ln(x_ref[...], o, gate_ref[...], g_ref[...], b_ref[...], alpha)


def _combine(ys, dest, x2, probs, mods, l, ln_g, ln_b, seq, alpha):
    n, d = x2.shape
    tc = 512
    n_tiles = n // tc
    per_b = seq // tc
    dest3 = dest.reshape(n_tiles, 1, tc * TOP_K)
    ln_g4 = ln_g.reshape(ln_g.shape[0], 2, 1, d)
    ln_b4 = ln_b.reshape(ln_b.shape[0], 2, 1, d)
    vec = lambda: pl.BlockSpec((None, None, 1, d), lambda i: (l, 1, 0, 0))
    return pl.pallas_call(
        functools.partial(_combine_kernel, tc=tc, alpha=alpha),
        grid=(n_tiles,),
        in_specs=[
            _first_tile_spec(tc * TOP_K), _next_tile_spec(n_tiles, tc * TOP_K),
            pl.BlockSpec((tc, d), lambda i: (i, 0)),
            pl.BlockSpec((tc, LANES), lambda i: (i, 0)),
            _mod_spec(mods, l, 1, GATE, lambda i: i // per_b),
            vec(), vec(),
            pl.BlockSpec(memory_space=pl.ANY),
        ],
        out_specs=pl.BlockSpec((tc, d), lambda i: (i, 0)),
        out_shape=jax.ShapeDtypeStruct((n, d), F32),
        scratch_shapes=[pltpu.VMEM((2, TOP_K, tc, d), F32), pltpu.SemaphoreType.DMA((2,))],
        compiler_params=_params(("arbitrary",)),
        name="combine",
    )(dest3, dest3, x2, probs, mods, ln_g4, ln_b4, ys)


def _routing_tables(top_i, cap, ch, static_modes, max_groups):
    n = top_i.shape[0]
    e_flat = top_i.reshape(-1)
    onehot = (e_flat[:, None] == jnp.arange(N_EXPERTS, dtype=jnp.int32)[None, :]).astype(jnp.int32)
    csum = jnp.cumsum(onehot, axis=0)
    counts = csum[-1]
    rank = jnp.take_along_axis(csum, e_flat[:, None], axis=1)[:, 0] - 1
    groups_per = (counts + cap - 1) // cap
    group_end = jnp.cumsum(groups_per)
    group_start = group_end - groups_per
    dest = (group_start * cap)[e_flat] + rank
    n_groups = group_end[-1]
    gidx = jnp.arange(max_groups, dtype=jnp.int32)
    gclamp = jnp.minimum(gidx, n_groups - 1)
    group_expert = jnp.sum((group_end[None, :] <= gclamp[:, None]).astype(jnp.int32), axis=1)
    rows = jnp.clip(counts[group_expert] - (gclamp - group_start[group_expert]) * cap, 0, cap)
    group_chunks = jnp.where(gidx < n_groups, (rows + ch - 1) // ch, 0)
    group_form = jnp.zeros_like(group_chunks)
    for i, (k, rpc) in reversed(list(enumerate(static_modes))):
        fits = jnp.logical_and(rows <= k * rpc, rows > (k - 1) * ch)
        group_form = jnp.where(fits, i + 1, group_form)
    p_rows = max_groups * cap
    tok_of_row = (jnp.arange(p_rows, dtype=jnp.int32) % n).at[dest].set(
        jnp.arange(n * TOP_K, dtype=jnp.int32) // TOP_K)
    assert cap % GATHER_TILE == 0
    tiles_per_group = cap // GATHER_TILE
    in_group = jnp.where(gidx < n_groups, rows, 0)[:, None] - (
        jnp.arange(tiles_per_group, dtype=jnp.int32) * GATHER_TILE)[None, :]
    tile_rows = jnp.clip(in_group, 0, GATHER_TILE).reshape(-1)
    tile_rows = (tile_rows + ROW_BURST - 1) // ROW_BURST * ROW_BURST
    return (dest.astype(jnp.int32), tok_of_row, tile_rows.astype(jnp.int32),
            group_expert.astype(jnp.int32), group_chunks.astype(jnp.int32), gclamp.astype(jnp.int32),
            n_groups.reshape(1).astype(jnp.int32), group_form.astype(jnp.int32))


def kernel(x, c, ada_w, ada_b, ln_g, ln_b, mix_w_in, conv_w, conv_b, lru_wa, lru_ba, lru_wx, lru_bx,
           lru_lam, pool_w, pool_b, pool_scale, mix_w_out, ffn_w_gate, ffn_w_up, ffn_w_down,
           router_w, exp_w_gate, exp_w_up, exp_w_down):
    nb, seq, d = x.shape
    depth = ada_w.shape[0]
    n = nb * seq
    alpha = float((2 * depth) ** 0.25)
    mods = _ada(c, ada_w, ada_b)
    x2 = x.reshape(n, d)

    row3 = lambda v: v.reshape(depth, 1, -1)
    seq_params = [conv_w, row3(conv_b), lru_wa.astype(BF16), row3(lru_ba), lru_wx.astype(BF16),
                  row3(lru_bx), row3(lru_lam), pool_w.astype(BF16), row3(pool_b), row3(pool_scale)]

    w_in_bf16 = _to_bf16(mix_w_in)

    ffn_tf, ffn_ch = 256, 256
    dense_cap = seq
    moe_cap = 2304
    moe_modes = ((moe_cap // ffn_ch - 1, ffn_ch), (moe_cap // ffn_ch, ffn_ch))
    for l in range(depth):
        moe = (l % 2 == 1)
        i = l // 2

        y = _mixer(x2, mods, l, w_in_bf16, seq_params, nb, seq)
        outs = _mix_out(y, x2, mods, l, ln_g, ln_b, mix_w_out, router_w[i] if moe else None,
                        seq, alpha)
        if not moe:
            x2, h = outs
            n_groups = n // dense_cap
            ge = jnp.full((n_groups,), i, jnp.int32)
            gc = jnp.full((n_groups,), dense_cap // ffn_ch, jnp.int32)
            gb = jnp.arange(n_groups, dtype=jnp.int32)
            gv = jnp.full((1,), n_groups, jnp.int32)
            gm = jnp.ones((n_groups,), jnp.int32)
            x2 = _ffn(h, ffn_w_gate, ffn_w_up, ffn_w_down, ge, gc, gb, gv, gm,
                      cap=dense_cap, tf=ffn_tf, ch=ffn_ch,
                      static_modes=((dense_cap // ffn_ch, ffn_ch),),
                      ln_args=(x2, mods, l, ln_g, ln_b), seq=seq, alpha=alpha)
        else:
            x_mix, h, idx, probs = outs
            top_i = idx[:, :TOP_K]
            n_exp = exp_w_gate.shape[1]
            f_exp = exp_w_gate.shape[-1]

            def moe(max_groups):
                dest, tok_of_row, tile_rows, ge, gc, gb, gv, gm = _routing_tables(
                    top_i, moe_cap, ffn_ch, moe_modes, max_groups)
                xs = _gather(h, tok_of_row, tile_rows, max_groups * moe_cap)
                ys = _ffn(xs,
                          exp_w_gate.reshape(-1, d, f_exp), exp_w_up.reshape(-1, d, f_exp),
                          exp_w_down.reshape(-1, f_exp, d), ge + i * n_exp, gc, gb, gv, gm,
                          cap=moe_cap, tf=ffn_tf, ch=ffn_ch, static_modes=moe_modes)
                return _combine(ys, dest, x_mix, probs, mods, l, ln_g, ln_b, seq, alpha)

            x2 = moe((n * TOP_K) // moe_cap + n_exp)
    return x2.reshape(nb, seq, d)
```

```python
import functools

import jax
import jax.numpy as jnp
from jax import lax
from jax.experimental import pallas as pl
from jax.experimental.pallas import tpu as pltpu

F32 = jnp.float32
BF16 = jnp.bfloat16

LN_EPS = 1e-5
LRU_C = 8.0
CONV_WIDTH = 4
LRU_HEADS = 4
POOL_WINDOWS = (2, 4, 8, 16)
N_EXPERTS = 8
TOP_K = 2

V7X_VMEM_LIMIT_BYTES = 58 * 1024 * 1024
LANES = 128
SUBLANES = 8
POOL_HIST = 16
SHIFT, SCALE, GATE = 0, 1, 2
GATHER_TILE = 768
ROW_BURST = 8


def _params(sem, vmem=V7X_VMEM_LIMIT_BYTES):
    return pltpu.CompilerParams(dimension_semantics=sem, vmem_limit_bytes=vmem)


def _mod_spec(mods, l, k, which, batch_of):
    d = mods.shape[-1]
    return pl.BlockSpec((None, None, None, None, 1, d),
                        lambda *g: (l, k, batch_of(*g), which, 0, 0))


def _layer_spec(arr, l):
    nd = arr.ndim - 1
    return pl.BlockSpec((None,) + arr.shape[1:], lambda *g: (l,) + (0,) * nd)


def _ada_kernel(c_ref, w_ref, b_ref, o_ref):
    c = c_ref[...]
    c_act = (c * jax.nn.sigmoid(c)).astype(BF16)
    o_ref[...] = jnp.dot(c_act, w_ref[...].astype(BF16),
                         preferred_element_type=F32) + b_ref[...]


def _ada(c, ada_w, ada_b):
    depth, _, d, d3 = ada_w.shape
    nb = c.shape[0]
    rows = -(-nb // SUBLANES) * SUBLANES
    c_pad = jnp.pad(c, ((0, rows - nb), (0, 0)))
    w = ada_w.reshape(depth * 2, d, d3)
    b = ada_b.reshape(depth * 2, 1, d3)
    tn = 1024
    out = pl.pallas_call(
        _ada_kernel,
        grid=(depth * 2, d3 // tn),
        in_specs=[
            pl.BlockSpec((rows, d), lambda l, j: (0, 0)),
            pl.BlockSpec((None, d, tn), lambda l, j: (l, 0, j)),
            pl.BlockSpec((None, 1, tn), lambda l, j: (l, 0, j)),
        ],
        out_specs=pl.BlockSpec((None, rows, tn), lambda l, j: (l, 0, j)),
        out_shape=jax.ShapeDtypeStruct((depth * 2, rows, d3), F32),
        compiler_params=_params(("arbitrary", "arbitrary")),
        name="ada",
    )(c_pad, w, b)
    return out.reshape(depth, 2, rows, 3, 1, d)


def _to_bf16_kernel(w_ref, o_ref):
    o_ref[...] = w_ref[...].astype(BF16)


def _to_bf16(w):
    shape = w.shape
    w2 = w.reshape(-1, shape[-1])
    tr = 512
    out = pl.pallas_call(
        _to_bf16_kernel,
        grid=(w2.shape[0] // tr,),
        in_specs=[pl.BlockSpec((tr, shape[-1]), lambda i: (i, 0))],
        out_specs=pl.BlockSpec((tr, shape[-1]), lambda i: (i, 0)),
        out_shape=jax.ShapeDtypeStruct(w2.shape, BF16),
        compiler_params=_params(("arbitrary",)),
        name="to_bf16",
    )(w2)
    return out.reshape(shape)


def _neg_expm1_nonpos(v, exp_v):
    poly = 1.0 / 120.0
    for coef in (1.0 / 24.0, 1.0 / 6.0, 0.5, 1.0):
        poly = poly * v + coef
    return jnp.where(v > -0.01, -v * poly, 1.0 - exp_v)


def _lru_scan(a, u, h_carry, cols, ts):
    row = lax.broadcasted_iota(jnp.int32, (ts, 1), 0) % SUBLANES
    d = 1
    while d < SUBLANES:
        keep = row >= d
        a_prev = jnp.where(keep, pltpu.roll(a, d, 0), 1.0)
        u_prev = jnp.where(keep, pltpu.roll(u, d, 0), 0.0)
        u = u + a * u_prev
        a = a * a_prev
        d *= 2
    carry = h_carry[0:1, cols]
    groups = []
    for g in range(ts // SUBLANES):
        rows = slice(g * SUBLANES, (g + 1) * SUBLANES)
        blk = a[rows] * carry + u[rows]
        groups.append(blk)
        carry = blk[SUBLANES - 1:SUBLANES]
    h_carry[:, cols] = jnp.broadcast_to(carry, (h_carry.shape[0], carry.shape[1]))
    return jnp.concatenate(groups, axis=0)


def _lru_head(z_ref, small, y_ref, conv_buf, h_carry, hd, ts):
    cw_ref, cb_ref, wa_ref, ba_ref, wx_ref, bx_ref, lam_ref = small[:7]
    d_lru = cw_ref.shape[-1]
    hdim = wa_ref.shape[-1]
    cols = slice(hd * hdim, (hd + 1) * hdim)

    xl = z_ref[:, cols]
    conv_buf[SUBLANES:, cols] = xl
    cw = cw_ref[:, cols]
    xc = cb_ref[:, cols] + cw[CONV_WIDTH - 1:CONV_WIDTH] * xl
    for k in range(CONV_WIDTH - 1):
        back = CONV_WIDTH - 1 - k
        xc = xc + cw[k:k + 1] * conv_buf[SUBLANES - back:SUBLANES - back + ts, cols]
    conv_buf[:SUBLANES, cols] = xl[ts - SUBLANES:]

    xcb = xc.astype(BF16)
    gate_a = jax.nn.sigmoid(
        jnp.dot(xcb, wa_ref[hd], preferred_element_type=F32) + ba_ref[:, cols])
    gate_x = jax.nn.sigmoid(
        jnp.dot(xcb, wx_ref[hd], preferred_element_type=F32) + bx_ref[:, cols])
    log_a = (-LRU_C) * gate_a * jax.nn.softplus(-lam_ref[:, cols])
    a = jnp.exp(log_a)
    m2 = _neg_expm1_nonpos(2.0 * log_a, a * a)
    mult = jnp.where(m2 > 0.0, m2 * lax.rsqrt(m2), 0.0)
    u = (xc * gate_x) * mult
    hs = _lru_scan(a, u, h_carry, cols, ts)
    gl = z_ref[:, d_lru + hd * hdim:d_lru + (hd + 1) * hdim]
    y_ref[:, cols] = (hs * jax.nn.gelu(gl)).astype(y_ref.dtype)


def _pool_group(z_ref, small, y_ref, pool_hist, gi, t, ts):
    d_lru = small[0].shape[-1]
    pw_ref, pb_ref, ps_ref = small[7:]
    gd = pw_ref.shape[-1]
    w = POOL_WINDOWS[gi]
    cols = slice(gi * gd, (gi + 1) * gd)
    row = lax.broadcasted_iota(jnp.int32, (ts, 1), 0)
    xp = z_ref[:, 2 * d_lru + gi * gd:2 * d_lru + (gi + 1) * gd]
    s = jnp.concatenate([pool_hist[:, cols], xp], axis=0)
    pool_hist[:, cols] = xp[ts - POOL_HIST:]
    sh = 1
    while sh < w:
        s = s + pltpu.roll(s, sh, 0)
        sh *= 2
    cnt = jnp.minimum(t * ts + row + 1, w).astype(F32)
    p = (s[POOL_HIST:] / cnt - xp).astype(BF16)
    yp = (jnp.dot(p, pw_ref[gi], preferred_element_type=F32) + pb_ref[:, cols]) * ps_ref[:, cols]
    y_ref[:, d_lru + gi * gd:d_lru + (gi + 1) * gd] = yp.astype(y_ref.dtype)


def _mixer_kernel(x_ref, shift_ref, scale_ref, w_ref, *rest, ts, nt):
    small = rest[:10]
    y_ref, z_even, z_odd, h_mod, conv_buf, pool_hist, h_carry = rest[10:]
    t = pl.program_id(1)
    n_parts = LRU_HEADS
    assert len(POOL_WINDOWS) == n_parts
    slab = w_ref.shape[1] // n_parts

    def modulate():
        h_mod[...] = (x_ref[...] * (1.0 + scale_ref[...]) + shift_ref[...]).astype(BF16)

    def part(i, z_dst, z_src):
        if z_dst is not None:
            cols = slice(i * slab, (i + 1) * slab)
            z_dst[:, cols] = jnp.dot(h_mod[...], w_ref[:, cols], preferred_element_type=F32)
        if z_src is not None:
            _lru_head(z_src, small, y_ref, conv_buf, h_carry, i, ts)
            _pool_group(z_src, small, y_ref, pool_hist, i, t - 1, ts)

    def parts(z_dst, z_src):
        for i in range(n_parts):
            part(i, z_dst, z_src)

    @pl.when(t == 0)
    def _():
        conv_buf[...] = jnp.zeros_like(conv_buf)
        pool_hist[...] = jnp.zeros_like(pool_hist)
        h_carry[...] = jnp.zeros_like(h_carry)
        modulate()
        parts(z_even, None)

    inner = jnp.logical_and(t > 0, t < nt)

    @pl.when(jnp.logical_and(inner, t % 2 == 1))
    def _():
        modulate()
        parts(z_odd, z_even)

    @pl.when(jnp.logical_and(inner, t % 2 == 0))
    def _():
        modulate()
        parts(z_even, z_odd)

    @pl.when(t == nt)
    def _():
        parts(None, z_odd if (nt - 1) % 2 == 1 else z_even)


def _mixer(x2, mods, l, w_in_bf16, small, nb, seq):
    n, d = x2.shape
    d_in = w_in_bf16.shape[-1]
    d_lru = small[0].shape[-1]
    assert d_in == 3 * d_lru
    ts = 256
    nt = seq // ts
    batch_of = lambda b, t: b
    return pl.pallas_call(
        functools.partial(_mixer_kernel, ts=ts, nt=nt),
        grid=(nb, nt + 1),
        in_specs=[
            pl.BlockSpec((ts, d), lambda b, t: (b * nt + jnp.minimum(t, nt - 1), 0)),
            _mod_spec(mods, l, 0, SHIFT, batch_of),
            _mod_spec(mods, l, 0, SCALE, batch_of),
            pl.BlockSpec((None, d, d_in), lambda b, t: (l, 0, 0), pipeline_mode=pl.Buffered(1)),
        ] + [_layer_spec(a, l) for a in small],
        out_specs=pl.BlockSpec((ts, 2 * d_lru), lambda b, t: (b * nt + jnp.maximum(t - 1, 0), 0)),
        out_shape=jax.ShapeDtypeStruct((n, 2 * d_lru), BF16),
        scratch_shapes=[pltpu.VMEM((ts, d_in), F32), pltpu.VMEM((ts, d_in), F32),
                        pltpu.VMEM((ts, d), BF16),
                        pltpu.VMEM((SUBLANES + ts, d_lru), F32),
                        pltpu.VMEM((POOL_HIST, d_lru), F32),
                        pltpu.VMEM((SUBLANES, d_lru), F32)],
        compiler_params=_params(("arbitrary", "arbitrary")),
        name="mixer",
    )(x2, mods, mods, w_in_bf16, *small)


def _resid_ln(x, o, gate, g, b, alpha):
    v = alpha * x + (1.0 + gate) * o
    mu = jnp.mean(v, axis=-1, keepdims=True)
    dv = v - mu
    var = jnp.mean(dv * dv, axis=-1, keepdims=True)
    return dv * lax.rsqrt(var + LN_EPS) * g + b


def _route(h, wr_ref):
    tm = h.shape[0]
    h_hi = h.astype(BF16)
    h_lo = (h - h_hi.astype(F32)).astype(BF16)
    r = jnp.dot(jnp.concatenate([h_hi, h_lo], axis=0), wr_ref[...], preferred_element_type=F32)
    r = r[:tm] + r[tm:]
    logits = r + pltpu.roll(r, LANES - N_EXPERTS, 1)
    lane = lax.broadcasted_iota(jnp.int32, logits.shape, 1).astype(F32)
    neg = jnp.float32(-jnp.inf)
    lg = jnp.where(lane < N_EXPERTS, logits, neg)
    m1 = jnp.max(lg, axis=-1, keepdims=True)
    i1 = jnp.min(jnp.where(lg == m1, lane, float(LANES)), axis=-1, keepdims=True)
    lg2 = jnp.where(lane == i1, neg, lg)
    m2 = jnp.max(lg2, axis=-1, keepdims=True)
    i2 = jnp.min(jnp.where(lg2 == m2, lane, float(LANES)), axis=-1, keepdims=True)
    e = jnp.exp(m2 - m1)
    p1 = 1.0 / (1.0 + e)
    p2 = e / (1.0 + e)
    idx = jnp.where(lane == 0, i1, i2).astype(jnp.int32)
    return idx, jnp.where(lane == 0, p1, p2)


def _router_weights(router_w):
    hi = router_w.astype(BF16)
    lo = (router_w - hi.astype(F32)).astype(BF16)
    wr = jnp.concatenate([hi, lo], axis=1)
    return jnp.pad(wr, ((0, 0), (0, LANES - wr.shape[1])))


def _mix_out_kernel(y_ref, x_ref, gate_ref, g_ref, b_ref, shift_ref, scale_ref, w_ref, *rest,
                    alpha, route):
    if route:
        wr_ref, xo_ref, h_ref, idx_ref, p_ref, w_scr = rest
    else:
        xo_ref, h_ref, w_scr = rest

    @pl.when(pl.program_id(0) == 0)
    def _():
        w_scr[...] = w_ref[...].astype(BF16)

    o = jnp.dot(y_ref[...], w_scr[...], preferred_element_type=F32)
    xn = _resid_ln(x_ref[...], o, gate_ref[...], g_ref[...], b_ref[...], alpha)
    xo_ref[...] = xn
    h = xn * (1.0 + scale_ref[...]) + shift_ref[...]
    h_ref[...] = h.astype(h_ref.dtype)
    if route:
        idx, p = _route(h, wr_ref)
        idx_ref[...] = idx
        p_ref[...] = p


def _mix_out(y, x2, mods, l, ln_g, ln_b, w_out_all, router_w, seq, alpha):
    n, d = x2.shape
    route = router_w is not None
    tm = 256
    per_b = seq // tm
    batch_of = lambda i: i // per_b
    tok = lambda: pl.BlockSpec((tm, d), lambda i: (i, 0))
    vec = lambda k: pl.BlockSpec((None, None, 1, d), lambda i: (l, k, 0, 0))
    ln_g4 = ln_g.reshape(ln_g.shape[0], 2, 1, d)
    ln_b4 = ln_b.reshape(ln_b.shape[0], 2, 1, d)
    in_specs = [tok(), tok(), _mod_spec(mods, l, 0, GATE, batch_of), vec(0), vec(0),
                _mod_spec(mods, l, 1, SHIFT, batch_of), _mod_spec(mods, l, 1, SCALE, batch_of),
                pl.BlockSpec((None, d, d), lambda i: (l, 0, 0), pipeline_mode=pl.Buffered(1))]
    args = [y, x2, mods, ln_g4, ln_b4, mods, mods, w_out_all]
    out_specs = [tok(), tok()]
    out_shape = [jax.ShapeDtypeStruct((n, d), F32),
                 jax.ShapeDtypeStruct((n, d), F32 if route else BF16)]
    if route:
        in_specs.append(pl.BlockSpec((d, LANES), lambda i: (0, 0)))
        args.append(_router_weights(router_w))
        out_specs += [pl.BlockSpec((tm, LANES), lambda i: (i, 0))] * 2
        out_shape += [jax.ShapeDtypeStruct((n, LANES), jnp.int32),
                      jax.ShapeDtypeStruct((n, LANES), F32)]
    return pl.pallas_call(
        functools.partial(_mix_out_kernel, alpha=alpha, route=route),
        grid=(n // tm,),
        in_specs=in_specs,
        out_specs=out_specs,
        out_shape=out_shape,
        scratch_shapes=[pltpu.VMEM((d, d), BF16)],
        compiler_params=_params(("arbitrary",)),
        name="mix_out",
    )(*args)


def _ffn_kernel(ge_ref, gc_ref, gb_ref, gv_ref, gm_ref, x_ref, wg0_ref, wu0_ref, wd0_ref, wgn_ref,
                wun_ref, wdn_ref, *rest, ch, alpha, ln, static_modes):
    if ln:
        xres_hbm, gate_ref, g_ref, b_ref, o_ref = rest[:5]
        bufs, (xr_buf, xr_sem) = rest[5:11], rest[11:]
    else:
        o_ref, bufs = rest[0], rest[1:7]
    w_bf16 = (bufs[:3], bufs[3:])
    s = pl.program_id(0)
    j = pl.program_id(1)
    nj = pl.num_programs(1)
    n = gc_ref[s]
    cap = x_ref.shape[0]

    @pl.when(j == 0)
    def _():
        o_ref[...] = jnp.zeros_like(o_ref)

    @pl.when(jnp.logical_and(s == 0, j == 0))
    def _():
        for dst, src in zip(w_bf16[0], (wg0_ref, wu0_ref, wd0_ref)):
            dst[...] = src[...].astype(BF16)

    def step(parity, mode):
        wg_s, wu_s, wd_s = w_bf16[parity]
        static_k, rows_per_chunk = (None, ch) if mode is None else mode

        def chunk_rows(c):
            start = c * rows_per_chunk
            return pl.ds(start if isinstance(c, int) else pl.multiple_of(start, ch),
                         rows_per_chunk)

        def up(c):
            x = x_ref[chunk_rows(c), :]
            g = jnp.dot(x, wg_s[...], preferred_element_type=F32)
            u = jnp.dot(x, wu_s[...], preferred_element_type=F32)
            return (g * jax.nn.sigmoid(g) * u).astype(BF16)

        def down(c, hmid):
            o_ref[chunk_rows(c), :] += jnp.dot(hmid, wd_s[...], preferred_element_type=F32)

        def fused(c, hmid):
            down(c - 1, hmid)
            return up(c)

        def convert_next(piece, n_pieces):
            for dst, src in zip(w_bf16[1 - parity], (wgn_ref, wun_ref, wdn_ref)):
                rows = src.shape[0] // n_pieces
                sl = slice(piece * rows, (piece + 1) * rows)
                dst[sl, :] = src[sl, :].astype(BF16)

        if static_k is not None and static_k >= 3:
            n_pieces = min(4, static_k - 2)
            pending = [up(0), up(1)]
            for c in range(2, static_k):
                down(c - 2, pending.pop(0))
                pending.append(up(c))
                if c - 2 < n_pieces:
                    convert_next(c - 2, n_pieces)
            down(static_k - 2, pending.pop(0))
            down(static_k - 1, pending.pop(0))
            return

        hmid = up(0)
        if static_k is not None:
            convert_next(0, 1)
            for c in range(1, static_k):
                hmid = fused(c, hmid)
            down(static_k - 1, hmid)
        else:
            convert_next(0, 1)
            pairs = lax.div(n - 1, 2)

            def two(p, hmid):
                return fused(2 * p + 2, fused(2 * p + 1, hmid))

            hmid = lax.fori_loop(0, pairs, two, hmid)
            hmid = lax.fori_loop(2 * pairs + 1, n, fused, hmid)
            down(n - 1, hmid)

    form = gm_ref[s]
    for parity in (0, 1):
        mine = jnp.logical_and(j % 2 == parity, n > 0)
        for i, mode in enumerate(static_modes):
            pl.when(jnp.logical_and(mine, form == i + 1))(functools.partial(step, parity, mode))
        pl.when(jnp.logical_and(mine, form == 0))(functools.partial(step, parity, None))

    if ln:
        row0 = gb_ref[s] * cap

        def xres_copy(c, slot):
            return pltpu.make_async_copy(
                xres_hbm.at[pl.ds(row0 + c * ch, ch)], xr_buf.at[slot], xr_sem.at[slot])

        @pl.when(jnp.logical_and(j == nj - 1, n > 0))
        def _():
            xres_copy(0, 0).start()

            def fin(c, carry):
                slot = c % 2

                @pl.when(c + 1 < n)
                def _():
                    xres_copy(c + 1, 1 - slot).start()

                xres_copy(c, slot).wait()
                rows = pl.ds(pl.multiple_of(c * ch, ch), ch)
                o_ref[rows, :] = _resid_ln(xr_buf[slot], o_ref[rows, :],
                                           gate_ref[...], g_ref[...], b_ref[...], alpha)
                return carry

            lax.fori_loop(0, n, fin, 0)


def _ffn(x_rows, w_gate, w_up, w_down, group_expert, group_chunks, group_block, groups_used,
         group_form, *, cap, tf, ch, static_modes, ln_args=None, seq=None, alpha=None):
    p_rows, d = x_rows.shape
    f = w_gate.shape[-1]
    n_groups = group_expert.shape[0]
    nj = f // tf
    assert nj % 2 == 0
    ln = ln_args is not None

    def following(s, j, ge, gv):
        wrap = j + 1 >= nj
        s2 = jnp.where(wrap, s + 1, s)
        j2 = jnp.where(wrap, 0, j + 1)
        live = s2 < gv[0]
        last = gv[0] - 1
        return ge[jnp.where(live, s2, last)], jnp.where(live, j2, nj - 1)

    def first(s, j, ge, gc, gb, gv, gm):
        return ge[0], 0, 0

    def next_up(s, j, ge, gc, gb, gv, gm):
        e, j2 = following(s, j, ge, gv)
        return e, 0, j2

    def next_down(s, j, ge, gc, gb, gv, gm):
        e, j2 = following(s, j, ge, gv)
        return e, j2, 0

    once = pl.Buffered(1)
    in_specs = [
        pl.BlockSpec((cap, d), lambda s, j, ge, gc, gb, gv, gm: (gb[s], 0), pipeline_mode=once),
        pl.BlockSpec((None, d, tf), first, pipeline_mode=once),
        pl.BlockSpec((None, d, tf), first, pipeline_mode=once),
        pl.BlockSpec((None, tf, d), first, pipeline_mode=once),
        pl.BlockSpec((None, d, tf), next_up),
        pl.BlockSpec((None, d, tf), next_up),
        pl.BlockSpec((None, tf, d), next_down),
    ]
    args = [x_rows, w_gate, w_up, w_down, w_gate, w_up, w_down]
    scratch = 2 * [pltpu.VMEM((d, tf), BF16), pltpu.VMEM((d, tf), BF16), pltpu.VMEM((tf, d), BF16)]
    if ln:
        xres, mods, l, ln_g, ln_b = ln_args
        assert seq % cap == 0
        per_b = seq // cap
        ln_g4 = ln_g.reshape(ln_g.shape[0], 2, 1, d)
        ln_b4 = ln_b.reshape(ln_b.shape[0], 2, 1, d)
        vec = lambda: pl.BlockSpec((None, None, 1, d), lambda s, j, *tables: (l, 1, 0, 0))
        in_specs += [
            pl.BlockSpec(memory_space=pl.ANY),
            _mod_spec(mods, l, 1, GATE, lambda s, j, ge, gc, gb, gv, gm: gb[s] // per_b),
            vec(), vec(),
        ]
        args += [xres, mods, ln_g4, ln_b4]
        scratch += [pltpu.VMEM((2, ch, d), F32), pltpu.SemaphoreType.DMA((2,))]
    grid_spec = pltpu.PrefetchScalarGridSpec(
        num_scalar_prefetch=5,
        grid=(n_groups, nj),
        in_specs=in_specs,
        out_specs=pl.BlockSpec((cap, d), lambda s, j, *tables: (s, 0), pipeline_mode=once),
        scratch_shapes=scratch,
    )
    return pl.pallas_call(
        functools.partial(_ffn_kernel, ch=ch, alpha=alpha, ln=ln, static_modes=static_modes),
        grid_spec=grid_spec,
        out_shape=jax.ShapeDtypeStruct((p_rows, d), F32),
        compiler_params=_params(("arbitrary", "arbitrary")),
        name="ffn_ln" if ln else "ffn_moe",
    )(group_expert, group_chunks, group_block, groups_used, group_form, *args)


def _row_copy(src_hbm, src_row, dst_buf, dst_row, sem):
    return pltpu.make_async_copy(src_hbm.at[pl.ds(src_row, 1)], dst_buf.at[pl.ds(dst_row, 1)], sem)


def _next_tile_spec(n_tiles, width):
    return pl.BlockSpec((None, 1, width), lambda i, *_: (jnp.minimum(i + 1, n_tiles - 1), 0, 0),
                        memory_space=pltpu.SMEM)


def _first_tile_spec(width):
    return pl.BlockSpec((None, 1, width), lambda i, *_: (0, 0, 0), memory_space=pltpu.SMEM)


def _gather_kernel(cnt_ref, tok0_ref, tokn_ref, h_hbm, o_ref, buf, sem, *, tg):
    i = pl.program_id(0)
    n = pl.num_programs(0)

    def bursts(tile):
        return lax.div(cnt_ref[tile], ROW_BURST)

    def issue(tok_ref, tile, slot):
        def body(g, carry):
            for k in range(ROW_BURST):
                r = g * ROW_BURST + k
                _row_copy(h_hbm, tok_ref[0, r], buf.at[slot], r, sem.at[slot]).start()
            return carry
        lax.fori_loop(0, bursts(tile), body, 0)

    def drain(tile, slot):
        def body(g, carry):
            for k in range(ROW_BURST):
                _row_copy(h_hbm, 0, buf.at[slot], g * ROW_BURST + k, sem.at[slot]).wait()
            return carry
        lax.fori_loop(0, bursts(tile), body, 0)

    @pl.when(i == 0)
    def _():
        buf[...] = jnp.zeros_like(buf)
        issue(tok0_ref, 0, 0)

    @pl.when(i + 1 < n)
    def _():
        issue(tokn_ref, i + 1, (i + 1) % 2)

    drain(i, i % 2)
    o_ref[...] = buf[i % 2].astype(o_ref.dtype)


def _gather(h, tok_of_row, tile_rows, p_rows):
    n, d = h.shape
    tg = GATHER_TILE
    n_tiles = p_rows // tg
    tok3 = tok_of_row.reshape(n_tiles, 1, tg)
    grid_spec = pltpu.PrefetchScalarGridSpec(
        num_scalar_prefetch=1,
        grid=(n_tiles,),
        in_specs=[_first_tile_spec(tg), _next_tile_spec(n_tiles, tg),
                  pl.BlockSpec(memory_space=pl.ANY)],
        out_specs=pl.BlockSpec((tg, d), lambda i, cnt: (i, 0)),
        scratch_shapes=[pltpu.VMEM((2, tg, d), F32), pltpu.SemaphoreType.DMA((2,))],
    )
    return pl.pallas_call(
        functools.partial(_gather_kernel, tg=tg),
        grid_spec=grid_spec,
        out_shape=jax.ShapeDtypeStruct((p_rows, d), BF16),
        compiler_params=_params(("arbitrary",)),
        name="gather",
    )(tile_rows, tok3, tok3, h)


def _combine_kernel(dst0_ref, dstn_ref, x_ref, p_ref, gate_ref, g_ref, b_ref, ys_hbm, o_ref,
                    buf, sem, *, tc, alpha):
    i = pl.program_id(0)
    n = pl.num_programs(0)

    def issue(dst_ref, slot):
        def body(r, carry):
            for k in range(TOP_K):
                _row_copy(ys_hbm, dst_ref[0, r * TOP_K + k], buf.at[slot, k], r,
                          sem.at[slot]).start()
            return carry
        lax.fori_loop(0, tc, body, 0, unroll=4)

    def drain(slot):
        def body(r, carry):
            for k in range(TOP_K):
                _row_copy(ys_hbm, 0, buf.at[slot, k], r, sem.at[slot]).wait()
            return carry
        lax.fori_loop(0, tc, body, 0, unroll=4)

    @pl.when(i == 0)
    def _():
        issue(dst0_ref, 0)

    @pl.when(i + 1 < n)
    def _():
        issue(dstn_ref, (i + 1) % 2)

    slot = i % 2
    drain(slot)
    p = p_ref[...]
    o = p[:, 0:1] * buf[slot, 0] + p[:, 1:2] * buf[slot, 1]
    o_ref[...] = _resid_ln(x_ref[...], o, gate_ref[...], g_ref[...], b_ref[...], alpha)


def _combine(ys, dest, x2, probs, mods, l, ln_g, ln_b, seq, alpha):
    n, d = x2.shape
    tc = 512
    n_tiles = n // tc
    per_b = seq // tc
    dest3 = dest.reshape(n_tiles, 1, tc * TOP_K)
    ln_g4 = ln_g.reshape(ln_g.shape[0], 2, 1, d)
    ln_b4 = ln_b.reshape(ln_b.shape[0], 2, 1, d)
    vec = lambda: pl.BlockSpec((None, None, 1, d), lambda i: (l, 1, 0, 0))
    return pl.pallas_call(
        functools.partial(_combine_kernel, tc=tc, alpha=alpha),
        grid=(n_tiles,),
        in_specs=[
            _first_tile_spec(tc * TOP_K), _next_tile_spec(n_tiles, tc * TOP_K),
            pl.BlockSpec((tc, d), lambda i: (i, 0)),
            pl.BlockSpec((tc, LANES), lambda i: (i, 0)),
            _mod_spec(mods, l, 1, GATE, lambda i: i // per_b),
            vec(), vec(),
            pl.BlockSpec(memory_space=pl.ANY),
        ],
        out_specs=pl.BlockSpec((tc, d), lambda i: (i, 0)),
        out_shape=jax.ShapeDtypeStruct((n, d), F32),
        scratch_shapes=[pltpu.VMEM((2, TOP_K, tc, d), F32), pltpu.SemaphoreType.DMA((2,))],
        compiler_params=_params(("arbitrary",)),
        name="combine",
    )(dest3, dest3, x2, probs, mods, ln_g4, ln_b4, ys)


def _routing_tables(top_i, cap, ch, static_modes, max_groups):
    n = top_i.shape[0]
    e_flat = top_i.reshape(-1)
    onehot = (e_flat[:, None] == jnp.arange(N_EXPERTS, dtype=jnp.int32)[None, :]).astype(jnp.int32)
    csum = jnp.cumsum(onehot, axis=0)
    counts = csum[-1]
    rank = jnp.take_along_axis(csum, e_flat[:, None], axis=1)[:, 0] - 1
    groups_per = (counts + cap - 1) // cap
    group_end = jnp.cumsum(groups_per)
    group_start = group_end - groups_per
    dest = (group_start * cap)[e_flat] + rank
    n_groups = group_end[-1]
    gidx = jnp.arange(max_groups, dtype=jnp.int32)
    gclamp = jnp.minimum(gidx, n_groups - 1)
    group_expert = jnp.sum((group_end[None, :] <= gclamp[:, None]).astype(jnp.int32), axis=1)
    rows = jnp.clip(counts[group_expert] - (gclamp - group_start[group_expert]) * cap, 0, cap)
    group_chunks = jnp.where(gidx < n_groups, (rows + ch - 1) // ch, 0)
    group_form = jnp.zeros_like(group_chunks)
    for i, (k, rpc) in reversed(list(enumerate(static_modes))):
        fits = jnp.logical_and(rows <= k * rpc, rows > (k - 1) * ch)
        group_form = jnp.where(fits, i + 1, group_form)
    p_rows = max_groups * cap
    order = jnp.argsort(e_flat, stable=True).astype(jnp.int32)
    count_start = jnp.cumsum(counts) - counts
    q = jnp.arange(p_rows, dtype=jnp.int32)
    q_expert = group_expert[q // cap]
    q_rank = q - group_start[q_expert] * cap
    routed = jnp.logical_and(q // cap < n_groups, q_rank < counts[q_expert])
    q_slot = order[jnp.clip(count_start[q_expert] + q_rank, 0, n * TOP_K - 1)]
    tok_of_row = jnp.where(routed, q_slot // TOP_K, q % n)
    assert cap % GATHER_TILE == 0
    tiles_per_group = cap // GATHER_TILE
    in_group = jnp.where(gidx < n_groups, rows, 0)[:, None] - (
        jnp.arange(tiles_per_group, dtype=jnp.int32) * GATHER_TILE)[None, :]
    tile_rows = jnp.clip(in_group, 0, GATHER_TILE).reshape(-1)
    tile_rows = (tile_rows + ROW_BURST - 1) // ROW_BURST * ROW_BURST
    return (dest.astype(jnp.int32), tok_of_row, tile_rows.astype(jnp.int32),
            group_expert.astype(jnp.int32), group_chunks.astype(jnp.int32), gclamp.astype(jnp.int32),
            n_groups.reshape(1).astype(jnp.int32), group_form.astype(jnp.int32))


def kernel(x, c, ada_w, ada_b, ln_g, ln_b, mix_w_in, conv_w, conv_b, lru_wa, lru_ba, lru_wx, lru_bx,
           lru_lam, pool_w, pool_b, pool_scale, mix_w_out, ffn_w_gate, ffn_w_up, ffn_w_down,
           router_w, exp_w_gate, exp_w_up, exp_w_down):
    nb, seq, d = x.shape
    depth = ada_w.shape[0]
    n = nb * seq
    alpha = float((2 * depth) ** 0.25)
    mods = _ada(c, ada_w, ada_b)
    x2 = x.reshape(n, d)

    row3 = lambda v: v.reshape(depth, 1, -1)
    seq_params = [conv_w, row3(conv_b), lru_wa.astype(BF16), row3(lru_ba), lru_wx.astype(BF16),
                  row3(lru_bx), row3(lru_lam), pool_w.astype(BF16), row3(pool_b), row3(pool_scale)]

    w_in_bf16 = _to_bf16(mix_w_in)

    ffn_tf, ffn_ch = 256, 256
    dense_cap = seq
    moe_cap = 2304
    moe_modes = ((moe_cap // ffn_ch - 1, ffn_ch), (moe_cap // ffn_ch, ffn_ch))
    for l in range(depth):
        moe = (l % 2 == 1)
        i = l // 2

        y = _mixer(x2, mods, l, w_in_bf16, seq_params, nb, seq)
        outs = _mix_out(y, x2, mods, l, ln_g, ln_b, mix_w_out, router_w[i] if moe else None,
                        seq, alpha)
        if not moe:
            x2, h = outs
            n_groups = n // dense_cap
            ge = jnp.full((n_groups,), i, jnp.int32)
            gc = jnp.full((n_groups,), dense_cap // ffn_ch, jnp.int32)
            gb = jnp.arange(n_groups, dtype=jnp.int32)
            gv = jnp.full((1,), n_groups, jnp.int32)
            gm = jnp.ones((n_groups,), jnp.int32)
            x2 = _ffn(h, ffn_w_gate, ffn_w_up, ffn_w_down, ge, gc, gb, gv, gm,
                      cap=dense_cap, tf=ffn_tf, ch=ffn_ch,
                      static_modes=((dense_cap // ffn_ch, ffn_ch),),
                      ln_args=(x2, mods, l, ln_g, ln_b), seq=seq, alpha=alpha)
        else:
            x_mix, h, idx, probs = outs
            top_i = idx[:, :TOP_K]
            n_exp = exp_w_gate.shape[1]
            f_exp = exp_w_gate.shape[-1]

            def moe(max_groups):
                dest, tok_of_row, tile_rows, ge, gc, gb, gv, gm = _routing_tables(
                    top_i, moe_cap, ffn_ch, moe_modes, max_groups)
                xs = _gather(h, tok_of_row, tile_rows, max_groups * moe_cap)
                ys = _ffn(xs,
                          exp_w_gate.reshape(-1, d, f_exp), exp_w_up.reshape(-1, d, f_exp),
                          exp_w_down.reshape(-1, f_exp, d), ge + i * n_exp, gc, gb, gv, gm,
                          cap=moe_cap, tf=ffn_tf, ch=ffn_ch, static_modes=moe_modes)
                return _combine(ys, dest, x_mix, probs, mods, l, ln_g, ln_b, seq, alpha)

            x2 = moe((n * TOP_K) // moe_cap + n_exp)
    return x2.reshape(nb, seq, d)
```
